```python
import math
import jax, jax.numpy as jnp
from jax import lax
import numpy as np

D_MODEL = 2048
BATCH = 4
SEQ = 2048
DEPTH = 1

CHUNK = 64
PLE_DIM = 256
D_MIX = D_MODEL
RET_WIDTH = D_MIX // 2
DIFF_WIDTH = D_MIX - RET_WIDTH
RET_HEADS = 8
RET_DV = RET_WIDTH // RET_HEADS
RET_DK = RET_DV // 2
DIFF_HEADS = 8
DIFF_DV = DIFF_WIDTH // DIFF_HEADS
DIFF_DH = DIFF_DV // 2
Q_BLOCK = 128
ROPE_BASE = 10000.0
EPS = 1e-6
RET_Q_COLS = RET_HEADS * RET_DK
RET_K_COLS = RET_HEADS * RET_DK
RET_V_COLS = RET_WIDTH
RET_G_COLS = RET_WIDTH
DIFF_Q_COLS = DIFF_HEADS * 2 * DIFF_DH
DIFF_K_COLS = DIFF_HEADS * 2 * DIFF_DH
DIFF_V_COLS = DIFF_WIDTH
DIFF_G_COLS = DIFF_WIDTH
D_IN = RET_Q_COLS + RET_K_COLS + RET_V_COLS + RET_G_COLS + DIFF_Q_COLS + DIFF_K_COLS + DIFF_V_COLS + DIFF_G_COLS

kernel_name = 'hybrid_retention_diffattn_layer'

F32 = jnp.float32


def rmsnorm(t, g):
    tf = t.astype(F32)
    out = tf * lax.rsqrt(jnp.mean(tf * tf, axis=-1, keepdims=True) + EPS) * g.astype(F32)
    return out.astype(t.dtype)


def rope(t, cos, sin):
    half = t.shape[-1] // 2
    t1, t2 = t[..., :half], t[..., half:]
    c = cos[None, :, None, :]
    s = sin[None, :, None, :]
    return jnp.concatenate([t1 * c - t2 * s, t1 * s + t2 * c], axis=-1)


def retention(q, k, v):
    b, s = q.shape[:2]
    n = s // CHUNK
    log_g = jnp.log1p(-jnp.exp2(-5.0 - jnp.arange(RET_HEADS, dtype=F32)))
    idx = jnp.arange(CHUNK, dtype=F32)
    d_intra = jnp.exp(jnp.abs(idx[:, None] - idx[None, :])[None] * log_g[:, None, None])
    xi = jnp.exp((idx + 1.0)[None, :] * log_g[:, None])
    zeta = jnp.exp((CHUNK - 1.0 - idx)[None, :] * log_g[:, None])
    g_chunk = jnp.exp(CHUNK * log_g)
    qc = q.astype(F32).reshape(b, n, CHUNK, RET_HEADS, RET_DK)
    kc = k.astype(F32).reshape(b, n, CHUNK, RET_HEADS, RET_DK)
    vc = v.astype(F32).reshape(b, n, CHUNK, RET_HEADS, RET_DV)
    scores = jnp.einsum('bnihd,bnjhd->bnhij', qc, kc) * d_intra
    intra = jnp.einsum('bnhij,bnjhe->bnihe', scores, vc)
    kv = jnp.einsum('bnjhd,bnjhe,hj->nbhde', kc, vc, zeta)

    def step(state, kv_n):
        return state * g_chunk[None, :, None, None] + kv_n, state

    _, r_prev = lax.scan(step, jnp.zeros((b, RET_HEADS, RET_DK, RET_DV), F32), kv)
    cross = jnp.einsum('bnihd,nbhde->bnihe', qc, r_prev) * xi.T[None, None, :, :, None]
    return (intra + cross).reshape(b, s, RET_HEADS, RET_DV)


def diff_attention(q, k, v, lam):
    b, s = q.shape[:2]
    nb = s // Q_BLOCK
    scale = DIFF_DH ** -0.5
    key_chunk = jnp.arange(s) // CHUNK
    kf = k.astype(F32)
    vf = v.astype(F32)
    qb = (q.astype(F32) * scale).reshape(b, nb, Q_BLOCK, DIFF_HEADS, 2, DIFF_DH).transpose(1, 0, 2, 3, 4, 5)

    def block(args):
        q_blk, blk = args
        q_chunk = (blk * Q_BLOCK + jnp.arange(Q_BLOCK)) // CHUNK
        mask = key_chunk[None, :] <= q_chunk[:, None]
        sc = jnp.einsum('bqhcd,bkhcd->bhcqk', q_blk, kf)
        sc = jnp.where(mask, sc, -jnp.inf)
        pr = jax.nn.softmax(sc, axis=-1)
        attn = pr[:, :, 0] - lam * pr[:, :, 1]
        return jnp.einsum('bhqk,bkhe->bqhe', attn, vf)

    out = lax.map(block, (qb, jnp.arange(nb)))
    return out.transpose(1, 0, 2, 3, 4).reshape(b, s, DIFF_HEADS, DIFF_DV)


def setup_inputs(seed: int = 0) -> dict:
    key = jax.random.key(seed)
    ks = jax.random.split(key, 20)
    nrm = lambda k_, shape: jax.random.normal(k_, shape, F32)
    return {
        'x': nrm(ks[0], (BATCH, SEQ, D_MODEL)),
        'p': nrm(ks[1], (DEPTH, BATCH, SEQ, PLE_DIM)),
        'attn_norm': 1.0 + 0.01 * nrm(ks[2], (DEPTH, D_MODEL)),
        'w_in': nrm(ks[3], (DEPTH, D_MODEL, D_IN)) * D_MODEL ** -0.5,
        'ret_gn': 1.0 + 0.01 * nrm(ks[4], (DEPTH, RET_WIDTH)),
        'diff_qn': 1.0 + 0.01 * nrm(ks[5], (DEPTH, DIFF_DH)),
        'diff_kn': 1.0 + 0.01 * nrm(ks[6], (DEPTH, DIFF_DH)),
        'diff_lq1': 0.1 * nrm(ks[7], (DEPTH, DIFF_DH)),
        'diff_lk1': 0.1 * nrm(ks[8], (DEPTH, DIFF_DH)),
        'diff_lq2': 0.1 * nrm(ks[9], (DEPTH, DIFF_DH)),
        'diff_lk2': 0.1 * nrm(ks[10], (DEPTH, DIFF_DH)),
        'diff_subln': 1.0 + 0.01 * nrm(ks[11], (DEPTH, DIFF_DV)),
        'w_out': nrm(ks[12], (DEPTH, D_MIX, D_MODEL)) * D_MIX ** -0.5,
        'ple_norm': 1.0 + 0.01 * nrm(ks[13], (DEPTH, D_MODEL)),
        'w_ple_gate': nrm(ks[14], (DEPTH, D_MODEL, D_MODEL)) * D_MODEL ** -0.5,
        'w_ple_proj': nrm(ks[15], (DEPTH, PLE_DIM, D_MODEL)) * PLE_DIM ** -0.5,
    }


def reference(x, p, attn_norm, w_in, ret_gn, diff_qn, diff_kn, diff_lq1, diff_lk1, diff_lq2, diff_lk2, diff_subln, w_out, ple_norm, w_ple_gate, w_ple_proj):
    b, s, _ = x.shape
    pos = jnp.arange(s, dtype=F32)
    inv_freq = ROPE_BASE ** (-jnp.arange(RET_DK // 2, dtype=F32) / (RET_DK // 2))
    ang = pos[:, None] * inv_freq[None, :]
    cos, sin = jnp.cos(ang), jnp.sin(ang)
    o1 = RET_Q_COLS
    o2 = o1 + RET_K_COLS
    o3 = o2 + RET_V_COLS
    o4 = o3 + RET_G_COLS
    o5 = o4 + DIFF_Q_COLS
    o6 = o5 + DIFF_K_COLS
    o7 = o6 + DIFF_V_COLS
    h = x
    for i in range(DEPTH):
        u = rmsnorm(h, attn_norm[i])
        z = u @ w_in[i]
        rq, rk, rv, rg, dq, dk, dv, dg = jnp.split(z, [o1, o2, o3, o4, o5, o6, o7], axis=-1)
        rq = rope(rq.reshape(b, s, RET_HEADS, RET_DK).astype(F32), cos, sin)
        rk = rope(rk.reshape(b, s, RET_HEADS, RET_DK).astype(F32), cos, sin) * RET_DK ** -0.5
        ro = retention(rq, rk, rv.reshape(b, s, RET_HEADS, RET_DV))
        ro = rmsnorm(ro, ret_gn[i].reshape(RET_HEADS, RET_DV)).reshape(b, s, RET_WIDTH)
        ro = ro * jax.nn.silu(rg.astype(F32))
        lam_init = 0.8 - 0.6 * math.exp(-0.3 * i)
        lam = (jnp.exp(jnp.sum(diff_lq1[i].astype(F32) * diff_lk1[i].astype(F32)))
               - jnp.exp(jnp.sum(diff_lq2[i].astype(F32) * diff_lk2[i].astype(F32))) + lam_init)
        dq = rmsnorm(dq.reshape(b, s, DIFF_HEADS, 2, DIFF_DH), diff_qn[i])
        dk = rmsnorm(dk.reshape(b, s, DIFF_HEADS, 2, DIFF_DH), diff_kn[i])
        do = diff_attention(dq, dk, dv.reshape(b, s, DIFF_HEADS, DIFF_DV), lam)
        do = rmsnorm(do, diff_subln[i]) * (1.0 - lam_init)
        do = do.reshape(b, s, DIFF_WIDTH) * jax.nn.silu(dg.astype(F32))
        mixed = jnp.concatenate([ro, do], axis=-1).astype(h.dtype)
        h = h + mixed @ w_out[i]
        gate = jax.nn.sigmoid(rmsnorm(h, ple_norm[i]) @ w_ple_gate[i])
        h = h + gate * (p[i] @ w_ple_proj[i])
    return h
```

```python
import functools
import math

import jax
import jax.numpy as jnp
from jax import lax
from jax.experimental import pallas as pl
from jax.experimental.pallas import tpu as pltpu

F32 = jnp.float32
BF16 = jnp.bfloat16

D_MODEL = 2048
CHUNK = 64
PLE_DIM = 256
RET_WIDTH = 1024
DIFF_WIDTH = 1024
RET_HEADS = 8
RET_DV = 128
RET_DK = 64
DIFF_HEADS = 8
DIFF_DV = 128
DIFF_DH = 64
ROPE_BASE = 10000.0
EPS = 1e-6
D_IN = 7168
LAM_INIT = 0.8 - 0.6 * math.exp(-0.3 * 0)

LANES = 128
VMEM_LIMIT_BYTES = 56 * 1024 * 1024

RQ_BLK = 0
RK_BLK = 512 // LANES
DQ_BLK = 3072 // LANES
DK_BLK = 4096 // LANES
DV_BLK = 5120 // LANES
DG_BLK = 6144 // LANES

PROJ_TM = 1024
PROJ_TN = 512
ROW_CHUNK = 256
RET_T = 256
ATT_TQ = 256
OUT_TM = 256


def _silu(a):
    return a * (1.0 / (1.0 + jnp.exp(-a)))


def _swap_halves(a):
    lane = lax.broadcasted_iota(jnp.int32, a.shape, 1)
    first = (lane % RET_DK) < (RET_DK // 2)
    return jnp.where(first, pltpu.roll(a, LANES - RET_DK // 2, 1), pltpu.roll(a, RET_DK // 2, 1))


def _proj_kernel(x_ref, g_ref, w_ref, cos_ref, sin_ref, qn_ref, kn_ref, gsum_ref, o_ref, u_ref, acc_ref):
    j = pl.program_id(1)
    tm = x_ref.shape[0]
    tn = w_ref.shape[1]

    @pl.when(j == 0)
    def _():
        for r in range(tm // ROW_CHUNK):
            rows = pl.ds(r * ROW_CHUNK, ROW_CHUNK)
            x = x_ref[rows, :]
            ms = jnp.mean(x * x, axis=-1, keepdims=True)
            u_ref[rows, :] = (x * lax.rsqrt(ms + EPS) * g_ref[...]).astype(BF16)

    acc_ref[...] = jnp.dot(u_ref[...], w_ref[...], preferred_element_type=F32)

    def rope_epilogue(scale):
        for r in range(tm // ROW_CHUNK):
            rows = pl.ds(r * ROW_CHUNK, ROW_CHUNK)
            c = cos_ref[rows, :]
            s = sin_ref[rows, :]
            for sl in range(tn // LANES):
                cols = pl.ds(sl * LANES, LANES)
                a = acc_ref[rows, cols]
                o_ref[rows, cols] = ((a * c + _swap_halves(a) * s) * scale).astype(BF16)

    def norm_epilogue(gn_ref, scale):
        for r in range(tm // ROW_CHUNK):
            rows = pl.ds(r * ROW_CHUNK, ROW_CHUNK)
            for sl in range(tn // 256):
                cols = pl.ds(sl * 256, 256)
                a = acc_ref[rows, cols]
                ms = jnp.dot((a * a).astype(BF16), gsum_ref[...], preferred_element_type=F32)
                o_ref[rows, cols] = (a * lax.rsqrt(ms + EPS) * gn_ref[...] * scale).astype(BF16)

    def silu_epilogue():
        for r in range(tm // ROW_CHUNK):
            rows = pl.ds(r * ROW_CHUNK, ROW_CHUNK)
            o_ref[rows, :] = _silu(acc_ref[rows, :]).astype(BF16)

    def plain_epilogue():
        o_ref[...] = acc_ref[...].astype(BF16)

    blocks_per_kilo = 1024 // tn
    seg = j // blocks_per_kilo

    @pl.when(j == 0)
    def _():
        rope_epilogue(1.0)

    @pl.when(j == 1)
    def _():
        rope_epilogue(RET_DK ** -0.5)

    @pl.when((seg == 1) | (seg == 5))
    def _():
        plain_epilogue()

    @pl.when((seg == 2) | (seg == 6))
    def _():
        silu_epilogue()

    @pl.when(seg == 3)
    def _():
        norm_epilogue(qn_ref, DIFF_DH ** -0.5)

    @pl.when(seg == 4)
    def _():
        norm_epilogue(kn_ref, 1.0)


def _retention_kernel(q_ref, k_ref, v_ref, g_ref, gn_ref, dmat_ref, xi_ref, zeta_ref, gdec_ref, o_ref, state_ref):
    s_len = q_ref.shape[0]
    lane = lax.broadcasted_iota(jnp.int32, (1, LANES), 1)
    state_ref[...] = jnp.zeros_like(state_ref)
    for n in range(s_len // RET_T):
        rows = pl.ds(n * RET_T, RET_T)
        q = q_ref[rows, :]
        k = k_ref[rows, :]
        v = v_ref[rows, :]
        cross = jnp.dot(q, state_ref[...].astype(BF16), preferred_element_type=F32)
        vz = (v.astype(F32) * zeta_ref[...]).astype(BF16)
        kv = lax.dot_general(k, vz, (((0,), (0,)), ((), ())), preferred_element_type=F32)
        state_ref[...] = state_ref[...] * gdec_ref[0] + kv * gdec_ref[1]
        for hh in range(2):
            cols = pl.ds(hh * RET_DV, RET_DV)
            qm = jnp.where((lane < RET_DK) == (hh == 0), q, jnp.zeros_like(q))
            sc = lax.dot_general(qm, k, (((1,), (1,)), ((), ())), preferred_element_type=F32)
            sc = sc * dmat_ref[hh]
            intra = jnp.dot(sc.astype(BF16), v[:, hh * RET_DV:(hh + 1) * RET_DV], preferred_element_type=F32)
            o = intra + cross[:, hh * RET_DV:(hh + 1) * RET_DV] * xi_ref[:, cols]
            ms = jnp.mean(o * o, axis=-1, keepdims=True)
            o = o * lax.rsqrt(ms + EPS) * gn_ref[:, cols]
            o_ref[rows, cols] = (o * g_ref[rows, cols].astype(F32)).astype(BF16)


def _diffattn_kernel(q_ref, k_ref, v_ref, g_ref, lq1_ref, lk1_ref, lq2_ref, lk2_ref, sub_ref, o_ref):
    qi = pl.program_id(2)
    tq = q_ref.shape[0]
    s_len = k_ref.shape[0]
    lam = (jnp.exp(jnp.sum(lq1_ref[...] * lk1_ref[...], axis=-1, keepdims=True))
           - jnp.exp(jnp.sum(lq2_ref[...] * lk2_ref[...], axis=-1, keepdims=True)) + LAM_INIT)
    q = q_ref[...]
    k = k_ref[...]
    v = v_ref[...]
    lane = lax.broadcasted_iota(jnp.int32, (1, LANES), 1)
    q_chunk = (qi * tq + lax.broadcasted_iota(jnp.int32, (tq, 1), 0)) // CHUNK
    k_chunk = lax.broadcasted_iota(jnp.int32, (1, s_len), 1) // CHUNK
    mask = k_chunk <= q_chunk
    outs = []
    for c in range(2):
        qc = jnp.where((lane < DIFF_DH) == (c == 0), q, jnp.zeros_like(q))
        sc = lax.dot_general(qc, k, (((1,), (1,)), ((), ())), preferred_element_type=F32)
        sc = jnp.where(mask, sc, -1e30)
        m = jnp.max(sc, axis=-1, keepdims=True)
        p = jnp.exp(sc - m)
        l = jnp.sum(p, axis=-1, keepdims=True)
        outs.append(jnp.dot(p.astype(BF16), v, preferred_element_type=F32) / l)
    o = outs[0] - lam * outs[1]
    ms = jnp.mean(o * o, axis=-1, keepdims=True)
    o = o * lax.rsqrt(ms + EPS) * sub_ref[...] * (1.0 - LAM_INIT)
    o_ref[...] = (o * g_ref[...].astype(F32)).astype(BF16)


def _out_kernel(x_ref, ro_ref, do_ref, p_ref, wo_ref, pn_ref, wg_ref, wp_ref, o_ref):
    acc = jnp.dot(ro_ref[...], wo_ref[0:RET_WIDTH, :], preferred_element_type=F32)
    acc = acc + jnp.dot(do_ref[...], wo_ref[RET_WIDTH:, :], preferred_element_type=F32)
    h1 = x_ref[...] + acc
    ms = jnp.mean(h1 * h1, axis=-1, keepdims=True)
    hn = (h1 * lax.rsqrt(ms + EPS) * pn_ref[...]).astype(BF16)
    z = jnp.dot(hn, wg_ref[...], preferred_element_type=F32)
    gate = 1.0 / (1.0 + jnp.exp(-z))
    ple = jnp.dot(p_ref[...].astype(BF16), wp_ref[...], preferred_element_type=F32)
    o_ref[...] = h1 + gate * ple


def _decay_tables():
    log_g = jnp.log1p(-jnp.exp2(-5.0 - jnp.arange(RET_HEADS, dtype=F32)))
    idx = jnp.arange(RET_T, dtype=F32)
    dist = jnp.abs(idx[:, None] - idx[None, :])
    same_or_earlier = (jnp.arange(RET_T)[None, :] // CHUNK) <= (jnp.arange(RET_T)[:, None] // CHUNK)
    dmat = jnp.where(same_or_earlier[None], jnp.exp(dist[None] * log_g[:, None, None]), 0.0)
    xi = jnp.exp((idx + 1.0)[None, :] * log_g[:, None])
    zeta = jnp.exp((RET_T - 1.0 - idx)[None, :] * log_g[:, None])
    g_step = jnp.exp(RET_T * log_g)
    n_pair = RET_HEADS // 2
    widen = lambda t: jnp.broadcast_to(t.reshape(n_pair, 2, RET_T, 1), (n_pair, 2, RET_T, RET_DV)) \
        .transpose(0, 2, 1, 3).reshape(n_pair, RET_T, 2 * RET_DV)
    xi_w = widen(xi)
    zeta_w = widen(zeta)
    rowh = (jnp.arange(2 * RET_DK) // RET_DK)[:, None]
    colh = (jnp.arange(2 * RET_DV) // RET_DV)[None, :]
    diag = (rowh == colh).astype(F32)
    gp = g_step.reshape(n_pair, 2)
    gq = jnp.where(colh[None] == 0, gp[:, 0, None, None], gp[:, 1, None, None]) * diag[None]
    gdec = jnp.stack([gq, jnp.broadcast_to(diag[None], gq.shape)], axis=1)
    return dmat, xi_w, zeta_w, gdec


def _rope_tables(s_len):
    pos = jnp.arange(s_len, dtype=F32)
    inv_freq = ROPE_BASE ** (-jnp.arange(RET_DK // 2, dtype=F32) / (RET_DK // 2))
    ang = pos[:, None] * inv_freq[None, :]
    cos, sin = jnp.cos(ang), jnp.sin(ang)
    cos_t = jnp.tile(cos, (1, LANES // (RET_DK // 2)))
    sin_t = jnp.tile(jnp.concatenate([-sin, sin], axis=-1), (1, LANES // RET_DK))
    return cos_t, sin_t


def kernel(x, p, attn_norm, w_in, ret_gn, diff_qn, diff_kn, diff_lq1, diff_lk1, diff_lq2, diff_lk2, diff_subln, w_out, ple_norm, w_ple_gate, w_ple_proj):
    b, s, d = x.shape
    m = b * s
    assert d == D_MODEL and s % PROJ_TM == 0 and m % OUT_TM == 0 and s % RET_T == 0 and s % ATT_TQ == 0
    x2 = x.reshape(m, d)
    p2 = p[0].reshape(m, PLE_DIM)

    cos_t, sin_t = _rope_tables(s)
    dmat, xi_w, zeta_w, gdec = _decay_tables()
    gidx = jnp.arange(256) // DIFF_DH
    gsum = jnp.where(gidx[:, None] == gidx[None, :], 1.0 / DIFF_DH, 0.0).astype(BF16)
    qn_t = jnp.tile(diff_qn[0], 256 // DIFF_DH).reshape(1, 256)
    kn_t = jnp.tile(diff_kn[0], 256 // DIFF_DH).reshape(1, 256)

    params = functools.partial(pltpu.CompilerParams, vmem_limit_bytes=VMEM_LIMIT_BYTES)
    const = lambda *_: (0, 0)
    pos_blocks = s // PROJ_TM

    z = pl.pallas_call(
        _proj_kernel,
        grid=(m // PROJ_TM, D_IN // PROJ_TN),
        in_specs=[
            pl.BlockSpec((PROJ_TM, d), lambda i, j: (i, 0)),
            pl.BlockSpec((1, d), const),
            pl.BlockSpec((d, PROJ_TN), lambda i, j: (0, j)),
            pl.BlockSpec((PROJ_TM, LANES), lambda i, j: (i % pos_blocks, 0)),
            pl.BlockSpec((PROJ_TM, LANES), lambda i, j: (i % pos_blocks, 0)),
            pl.BlockSpec((1, 256), const),
            pl.BlockSpec((1, 256), const),
            pl.BlockSpec((256, 256), const),
        ],
        out_specs=pl.BlockSpec((PROJ_TM, PROJ_TN), lambda i, j: (i, j)),
        out_shape=jax.ShapeDtypeStruct((m, D_IN), BF16),
        scratch_shapes=[pltpu.VMEM((PROJ_TM, d), BF16), pltpu.VMEM((PROJ_TM, PROJ_TN), F32)],
        compiler_params=params(dimension_semantics=("arbitrary", "arbitrary")),
        name="proj",
    )(x2, attn_norm[0].reshape(1, d), w_in[0].astype(BF16), cos_t, sin_t, qn_t, kn_t, gsum)

    z3 = z.reshape(b, s, D_IN)
    n_pair = RET_HEADS // 2
    ro = pl.pallas_call(
        _retention_kernel,
        grid=(b, n_pair),
        in_specs=[
            pl.BlockSpec((None, s, LANES), lambda bi, hp: (bi, 0, RQ_BLK + hp)),
            pl.BlockSpec((None, s, LANES), lambda bi, hp: (bi, 0, RK_BLK + hp)),
            pl.BlockSpec((None, s, 2 * RET_DV), lambda bi, hp: (bi, 0, 1024 // 256 + hp)),
            pl.BlockSpec((None, s, 2 * RET_DV), lambda bi, hp: (bi, 0, 2048 // 256 + hp)),
            pl.BlockSpec((1, 2 * RET_DV), lambda bi, hp: (0, hp)),
            pl.BlockSpec((2, RET_T, RET_T), lambda bi, hp: (hp, 0, 0)),
            pl.BlockSpec((None, RET_T, 2 * RET_DV), lambda bi, hp: (hp, 0, 0)),
            pl.BlockSpec((None, RET_T, 2 * RET_DV), lambda bi, hp: (hp, 0, 0)),
            pl.BlockSpec((None, 2, 2 * RET_DK, 2 * RET_DV), lambda bi, hp: (hp, 0, 0, 0)),
        ],
        out_specs=pl.BlockSpec((None, s, 2 * RET_DV), lambda bi, hp: (bi, 0, hp)),
        out_shape=jax.ShapeDtypeStruct((b, s, RET_WIDTH), BF16),
        scratch_shapes=[pltpu.VMEM((2 * RET_DK, 2 * RET_DV), F32)],
        compiler_params=params(dimension_semantics=("arbitrary", "arbitrary")),
        name="retention",
    )(z3, z3, z3, z3, ret_gn[0].reshape(1, RET_WIDTH), dmat, xi_w, zeta_w, gdec)

    vec = lambda a: a[0].reshape(1, -1)
    do = pl.pallas_call(
        _diffattn_kernel,
        grid=(b, DIFF_HEADS, s // ATT_TQ),
        in_specs=[
            pl.BlockSpec((None, ATT_TQ, LANES), lambda bi, h, qi: (bi, qi, DQ_BLK + h)),
            pl.BlockSpec((None, s, LANES), lambda bi, h, qi: (bi, 0, DK_BLK + h)),
            pl.BlockSpec((None, s, LANES), lambda bi, h, qi: (bi, 0, DV_BLK + h)),
            pl.BlockSpec((None, ATT_TQ, LANES), lambda bi, h, qi: (bi, qi, DG_BLK + h)),
            pl.BlockSpec((1, DIFF_DH), lambda *_: (0, 0)),
            pl.BlockSpec((1, DIFF_DH), lambda *_: (0, 0)),
            pl.BlockSpec((1, DIFF_DH), lambda *_: (0, 0)),
            pl.BlockSpec((1, DIFF_DH), lambda *_: (0, 0)),
            pl.BlockSpec((1, DIFF_DV), lambda *_: (0, 0)),
        ],
        out_specs=pl.BlockSpec((None, ATT_TQ, LANES), lambda bi, h, qi: (bi, qi, h)),
        out_shape=jax.ShapeDtypeStruct((b, s, DIFF_WIDTH), BF16),
        compiler_params=params(dimension_semantics=("arbitrary", "arbitrary", "arbitrary")),
        name="diffattn",
    )(z3, z3, z3, z3, vec(diff_lq1), vec(diff_lk1), vec(diff_lq2), vec(diff_lk2), vec(diff_subln))

    resident = functools.partial(pl.BlockSpec, pipeline_mode=pl.Buffered(1))
    out = pl.pallas_call(
        _out_kernel,
        grid=(m // OUT_TM,),
        in_specs=[
            pl.BlockSpec((OUT_TM, d), lambda i: (i, 0)),
            pl.BlockSpec((OUT_TM, RET_WIDTH), lambda i: (i, 0)),
            pl.BlockSpec((OUT_TM, DIFF_WIDTH), lambda i: (i, 0)),
            pl.BlockSpec((OUT_TM, PLE_DIM), lambda i: (i, 0)),
            resident((d, d), lambda i: (0, 0)),
            pl.BlockSpec((1, d), lambda i: (0, 0)),
            resident((d, d), lambda i: (0, 0)),
            resident((PLE_DIM, d), lambda i: (0, 0)),
        ],
        out_specs=pl.BlockSpec((OUT_TM, d), lambda i: (i, 0)),
        out_shape=jax.ShapeDtypeStruct((m, d), F32),
        compiler_params=params(dimension_semantics=("arbitrary",)),
        name="outproj",
    )(x2, ro.reshape(m, RET_WIDTH), do.reshape(m, DIFF_WIDTH), p2, w_out[0].astype(BF16),
      ple_norm[0].reshape(1, d), w_ple_gate[0].astype(BF16), w_ple_proj[0].astype(BF16))

    return out.reshape(b, s, d)
```

```python
import functools
import math

import jax
import jax.numpy as jnp
from jax import lax
from jax.experimental import pallas as pl
from jax.experimental.pallas import tpu as pltpu

F32 = jnp.float32
BF16 = jnp.bfloat16

D_MODEL = 2048
CHUNK = 64
PLE_DIM = 256
RET_WIDTH = 1024
DIFF_WIDTH = 1024
RET_HEADS = 8
RET_DV = 128
RET_DK = 64
DIFF_HEADS = 8
DIFF_DV = 128
DIFF_DH = 64
ROPE_BASE = 10000.0
EPS = 1e-6
D_IN = 7168
LAM_INIT = 0.8 - 0.6 * math.exp(-0.3 * 0)

LANES = 128
VMEM_LIMIT_BYTES = 56 * 1024 * 1024

RQ_BLK = 0
RK_BLK = 512 // LANES
DQ_BLK = 3072 // LANES
DK_BLK = 4096 // LANES
DV_BLK = 5120 // LANES
DG_BLK = 6144 // LANES

PROJ_TM = 1024
PROJ_TN = 512
ROW_CHUNK = 256
RET_T = 256
ATT_TQ = 256
ATT_TK = 256
OUT_TM = 256
LOG2E = math.log2(math.e)
NEG_BIG = -1e30


def _silu(a):
    return a * (1.0 / (1.0 + jnp.exp(-a)))


def _swap_halves(a):
    lane = lax.broadcasted_iota(jnp.int32, a.shape, 1)
    first = (lane % RET_DK) < (RET_DK // 2)
    return jnp.where(first, pltpu.roll(a, LANES - RET_DK // 2, 1), pltpu.roll(a, RET_DK // 2, 1))


def _proj_kernel(x_ref, g_ref, w_ref, cos_ref, sin_ref, qn_ref, kn_ref, gsum_ref, o_ref, u_ref, acc_ref):
    j = pl.program_id(1)
    tm = x_ref.shape[0]
    tn = w_ref.shape[1]

    @pl.when(j == 0)
    def _():
        for r in range(tm // ROW_CHUNK):
            rows = pl.ds(r * ROW_CHUNK, ROW_CHUNK)
            x = x_ref[rows, :]
            ms = jnp.mean(x * x, axis=-1, keepdims=True)
            u_ref[rows, :] = (x * lax.rsqrt(ms + EPS) * g_ref[...]).astype(BF16)

    acc_ref[...] = jnp.dot(u_ref[...], w_ref[...], preferred_element_type=F32)

    def rope_epilogue(scale):
        for r in range(tm // ROW_CHUNK):
            rows = pl.ds(r * ROW_CHUNK, ROW_CHUNK)
            c = cos_ref[rows, :]
            s = sin_ref[rows, :]
            for sl in range(tn // LANES):
                cols = pl.ds(sl * LANES, LANES)
                a = acc_ref[rows, cols]
                o_ref[rows, cols] = ((a * c + _swap_halves(a) * s) * scale).astype(BF16)

    def norm_epilogue(gn_ref, scale):
        for r in range(tm // ROW_CHUNK):
            rows = pl.ds(r * ROW_CHUNK, ROW_CHUNK)
            for sl in range(tn // 256):
                cols = pl.ds(sl * 256, 256)
                a = acc_ref[rows, cols]
                ms = jnp.dot((a * a).astype(BF16), gsum_ref[...], preferred_element_type=F32)
                o_ref[rows, cols] = (a * lax.rsqrt(ms + EPS) * gn_ref[...] * scale).astype(BF16)

    def silu_epilogue():
        for r in range(tm // ROW_CHUNK):
            rows = pl.ds(r * ROW_CHUNK, ROW_CHUNK)
            o_ref[rows, :] = _silu(acc_ref[rows, :]).astype(BF16)

    def plain_epilogue():
        o_ref[...] = acc_ref[...].astype(BF16)

    blocks_per_kilo = 1024 // tn
    seg = j // blocks_per_kilo

    @pl.when(j == 0)
    def _():
        rope_epilogue(1.0)

    @pl.when(j == 1)
    def _():
        rope_epilogue(RET_DK ** -0.5)

    @pl.when((seg == 1) | (seg == 5))
    def _():
        plain_epilogue()

    @pl.when((seg == 2) | (seg == 6))
    def _():
        silu_epilogue()

    @pl.when(seg == 3)
    def _():
        norm_epilogue(qn_ref, DIFF_DH ** -0.5 * LOG2E)

    @pl.when(seg == 4)
    def _():
        norm_epilogue(kn_ref, 1.0)


def _retention_kernel(q_ref, k_ref, v_ref, g_ref, gn_ref, dmat_ref, xi_ref, zeta_ref, gdec_ref, o_ref, state_ref):
    s_len = q_ref.shape[0]
    lane = lax.broadcasted_iota(jnp.int32, (1, LANES), 1)
    state_ref[...] = jnp.zeros_like(state_ref)
    for n in range(s_len // RET_T):
        rows = pl.ds(n * RET_T, RET_T)
        q = q_ref[rows, :]
        k = k_ref[rows, :]
        v = v_ref[rows, :]
        cross = jnp.dot(q, state_ref[...].astype(BF16), preferred_element_type=F32)
        vz = (v.astype(F32) * zeta_ref[...]).astype(BF16)
        kv = lax.dot_general(k, vz, (((0,), (0,)), ((), ())), preferred_element_type=F32)
        state_ref[...] = state_ref[...] * gdec_ref[0] + kv * gdec_ref[1]
        for hh in range(2):
            cols = pl.ds(hh * RET_DV, RET_DV)
            qm = jnp.where((lane < RET_DK) == (hh == 0), q, jnp.zeros_like(q))
            sc = lax.dot_general(qm, k, (((1,), (1,)), ((), ())), preferred_element_type=F32)
            sc = sc * dmat_ref[hh]
            intra = jnp.dot(sc.astype(BF16), v[:, hh * RET_DV:(hh + 1) * RET_DV], preferred_element_type=F32)
            o = intra + cross[:, hh * RET_DV:(hh + 1) * RET_DV] * xi_ref[:, cols]
            ms = jnp.mean(o * o, axis=-1, keepdims=True)
            o = o * lax.rsqrt(ms + EPS) * gn_ref[:, cols]
            o_ref[rows, cols] = (o * g_ref[rows, cols].astype(F32)).astype(BF16)


def _diffattn_kernel(q_ref, k_ref, v_ref, g_ref, bias_ref, lq1_ref, lk1_ref, lq2_ref, lk2_ref, sub_ref, o_ref,
                     vt_ref):
    s_len = k_ref.shape[0]
    tq = ATT_TQ
    for t in range(s_len // ATT_TK):
        cols = pl.ds(t * ATT_TK, ATT_TK)
        vt_ref[:, cols] = v_ref[cols, :].astype(F32).T.astype(BF16)

    lam = (jnp.exp(jnp.sum(lq1_ref[...] * lk1_ref[...], axis=-1, keepdims=True))
           - jnp.exp(jnp.sum(lq2_ref[...] * lk2_ref[...], axis=-1, keepdims=True)) + LAM_INIT)
    for qi in range(s_len // tq):
        rows = pl.ds(qi * tq, tq)
        lo = qi * tq
        qt = q_ref[rows, :].astype(F32).T
        row = lax.broadcasted_iota(jnp.int32, qt.shape, 0)
        qbd = jnp.concatenate([jnp.where(row < DIFF_DH, qt, 0.0), jnp.where(row >= DIFF_DH, qt, 0.0)],
                              axis=1).astype(BF16)
        sd = jnp.dot(k_ref[lo:lo + tq, :], qbd, preferred_element_type=F32) + bias_ref[...]
        m = jnp.max(sd, axis=0, keepdims=True)
        if qi > 0:
            so = jnp.dot(k_ref[0:lo, :], qbd, preferred_element_type=F32)
            m = jnp.maximum(m, jnp.max(so, axis=0, keepdims=True))
        pd = jnp.exp2(sd - m)
        l = jnp.sum(pd, axis=0, keepdims=True)
        p = pd.astype(BF16)
        if qi > 0:
            po = jnp.exp2(so - m)
            l = l + jnp.sum(po, axis=0, keepdims=True)
            p = jnp.concatenate([po.astype(BF16), p], axis=0)
        on = jnp.dot(vt_ref[:, 0:lo + tq], p, preferred_element_type=F32) / l
        o = (on[:, 0:tq] - lam * on[:, tq:2 * tq]).T
        ms_o = jnp.mean(o * o, axis=-1, keepdims=True)
        o = o * lax.rsqrt(ms_o + EPS) * sub_ref[...] * (1.0 - LAM_INIT)
        o_ref[rows, :] = (o * g_ref[rows, :].astype(F32)).astype(BF16)


def _out_kernel(x_ref, ro_ref, do_ref, p_ref, wo_ref, pn_ref, wg_ref, wp_ref, o_ref):
    acc = jnp.dot(ro_ref[...], wo_ref[0:RET_WIDTH, :], preferred_element_type=F32)
    acc = acc + jnp.dot(do_ref[...], wo_ref[RET_WIDTH:, :], preferred_element_type=F32)
    h1 = x_ref[...] + acc
    ms = jnp.mean(h1 * h1, axis=-1, keepdims=True)
    hn = (h1 * lax.rsqrt(ms + EPS) * pn_ref[...]).astype(BF16)
    z = jnp.dot(hn, wg_ref[...], preferred_element_type=F32)
    gate = 1.0 / (1.0 + jnp.exp(-z))
    ple = jnp.dot(p_ref[...].astype(BF16), wp_ref[...], preferred_element_type=F32)
    o_ref[...] = h1 + gate * ple


def _decay_tables():
    log_g = jnp.log1p(-jnp.exp2(-5.0 - jnp.arange(RET_HEADS, dtype=F32)))
    idx = jnp.arange(RET_T, dtype=F32)
    dist = jnp.abs(idx[:, None] - idx[None, :])
    same_or_earlier = (jnp.arange(RET_T)[None, :] // CHUNK) <= (jnp.arange(RET_T)[:, None] // CHUNK)
    dmat = jnp.where(same_or_earlier[None], jnp.exp(dist[None] * log_g[:, None, None]), 0.0)
    xi = jnp.exp((idx + 1.0)[None, :] * log_g[:, None])
    zeta = jnp.exp((RET_T - 1.0 - idx)[None, :] * log_g[:, None])
    g_step = jnp.exp(RET_T * log_g)
    n_pair = RET_HEADS // 2
    widen = lambda t: jnp.broadcast_to(t.reshape(n_pair, 2, RET_T, 1), (n_pair, 2, RET_T, RET_DV)) \
        .transpose(0, 2, 1, 3).reshape(n_pair, RET_T, 2 * RET_DV)
    xi_w = widen(xi)
    zeta_w = widen(zeta)
    rowh = (jnp.arange(2 * RET_DK) // RET_DK)[:, None]
    colh = (jnp.arange(2 * RET_DV) // RET_DV)[None, :]
    diag = (rowh == colh).astype(F32)
    gp = g_step.reshape(n_pair, 2)
    gq = jnp.where(colh[None] == 0, gp[:, 0, None, None], gp[:, 1, None, None]) * diag[None]
    gdec = jnp.stack([gq, jnp.broadcast_to(diag[None], gq.shape)], axis=1)
    return dmat, xi_w, zeta_w, gdec


def _rope_tables(s_len):
    pos = jnp.arange(s_len, dtype=F32)
    inv_freq = ROPE_BASE ** (-jnp.arange(RET_DK // 2, dtype=F32) / (RET_DK // 2))
    ang = pos[:, None] * inv_freq[None, :]
    cos, sin = jnp.cos(ang), jnp.sin(ang)
    cos_t = jnp.tile(cos, (1, LANES // (RET_DK // 2)))
    sin_t = jnp.tile(jnp.concatenate([-sin, sin], axis=-1), (1, LANES // RET_DK))
    return cos_t, sin_t


def kernel(x, p, attn_norm, w_in, ret_gn, diff_qn, diff_kn, diff_lq1, diff_lk1, diff_lq2, diff_lk2, diff_subln, w_out, ple_norm, w_ple_gate, w_ple_proj):
    b, s, d = x.shape
    m = b * s
    assert d == D_MODEL and s % PROJ_TM == 0 and m % OUT_TM == 0 and s % RET_T == 0 and s % ATT_TQ == 0
    x2 = x.reshape(m, d)
    p2 = p[0].reshape(m, PLE_DIM)

    cos_t, sin_t = _rope_tables(s)
    dmat, xi_w, zeta_w, gdec = _decay_tables()
    gidx = jnp.arange(256) // DIFF_DH
    gsum = jnp.where(gidx[:, None] == gidx[None, :], 1.0 / DIFF_DH, 0.0).astype(BF16)
    qn_t = jnp.tile(diff_qn[0], 256 // DIFF_DH).reshape(1, 256)
    kn_t = jnp.tile(diff_kn[0], 256 // DIFF_DH).reshape(1, 256)

    params = functools.partial(pltpu.CompilerParams, vmem_limit_bytes=VMEM_LIMIT_BYTES)
    const = lambda *_: (0, 0)
    pos_blocks = s // PROJ_TM

    z = pl.pallas_call(
        _proj_kernel,
        grid=(m // PROJ_TM, D_IN // PROJ_TN),
        in_specs=[
            pl.BlockSpec((PROJ_TM, d), lambda i, j: (i, 0)),
            pl.BlockSpec((1, d), const),
            pl.BlockSpec((d, PROJ_TN), lambda i, j: (0, j)),
            pl.BlockSpec((PROJ_TM, LANES), lambda i, j: (i % pos_blocks, 0)),
            pl.BlockSpec((PROJ_TM, LANES), lambda i, j: (i % pos_blocks, 0)),
            pl.BlockSpec((1, 256), const),
            pl.BlockSpec((1, 256), const),
            pl.BlockSpec((256, 256), const),
        ],
        out_specs=pl.BlockSpec((PROJ_TM, PROJ_TN), lambda i, j: (i, j)),
        out_shape=jax.ShapeDtypeStruct((m, D_IN), BF16),
        scratch_shapes=[pltpu.VMEM((PROJ_TM, d), BF16), pltpu.VMEM((PROJ_TM, PROJ_TN), F32)],
        compiler_params=params(dimension_semantics=("arbitrary", "arbitrary")),
        name="proj",
    )(x2, attn_norm[0].reshape(1, d), w_in[0].astype(BF16), cos_t, sin_t, qn_t, kn_t, gsum)

    z3 = z.reshape(b, s, D_IN)
    n_pair = RET_HEADS // 2
    ro = pl.pallas_call(
        _retention_kernel,
        grid=(b, n_pair),
        in_specs=[
            pl.BlockSpec((None, s, LANES), lambda bi, hp: (bi, 0, RQ_BLK + hp)),
            pl.BlockSpec((None, s, LANES), lambda bi, hp: (bi, 0, RK_BLK + hp)),
            pl.BlockSpec((None, s, 2 * RET_DV), lambda bi, hp: (bi, 0, 1024 // 256 + hp)),
            pl.BlockSpec((None, s, 2 * RET_DV), lambda bi, hp: (bi, 0, 2048 // 256 + hp)),
            pl.BlockSpec((1, 2 * RET_DV), lambda bi, hp: (0, hp)),
            pl.BlockSpec((2, RET_T, RET_T), lambda bi, hp: (hp, 0, 0)),
            pl.BlockSpec((None, RET_T, 2 * RET_DV), lambda bi, hp: (hp, 0, 0)),
            pl.BlockSpec((None, RET_T, 2 * RET_DV), lambda bi, hp: (hp, 0, 0)),
            pl.BlockSpec((None, 2, 2 * RET_DK, 2 * RET_DV), lambda bi, hp: (hp, 0, 0, 0)),
        ],
        out_specs=pl.BlockSpec((None, s, 2 * RET_DV), lambda bi, hp: (bi, 0, hp)),
        out_shape=jax.ShapeDtypeStruct((b, s, RET_WIDTH), BF16),
        scratch_shapes=[pltpu.VMEM((2 * RET_DK, 2 * RET_DV), F32)],
        compiler_params=params(dimension_semantics=("arbitrary", "arbitrary")),
        name="retention",
    )(z3, z3, z3, z3, ret_gn[0].reshape(1, RET_WIDTH), dmat, xi_w, zeta_w, gdec)

    vec = lambda a: a[0].reshape(1, -1)
    tile_chunk = jnp.arange(ATT_TK) // CHUNK
    diag_bias = jnp.where(tile_chunk[:, None] <= tile_chunk[None, :], 0.0, NEG_BIG).astype(F32)
    diag_bias = jnp.tile(diag_bias, (1, 2))
    do = pl.pallas_call(
        _diffattn_kernel,
        grid=(b, DIFF_HEADS),
        in_specs=[
            pl.BlockSpec((None, s, LANES), lambda bi, h: (bi, 0, DQ_BLK + h)),
            pl.BlockSpec((None, s, LANES), lambda bi, h: (bi, 0, DK_BLK + h)),
            pl.BlockSpec((None, s, LANES), lambda bi, h: (bi, 0, DV_BLK + h)),
            pl.BlockSpec((None, s, LANES), lambda bi, h: (bi, 0, DG_BLK + h)),
            pl.BlockSpec((ATT_TK, 2 * ATT_TQ), lambda *_: (0, 0)),
            pl.BlockSpec((1, DIFF_DH), lambda *_: (0, 0)),
            pl.BlockSpec((1, DIFF_DH), lambda *_: (0, 0)),
            pl.BlockSpec((1, DIFF_DH), lambda *_: (0, 0)),
            pl.BlockSpec((1, DIFF_DH), lambda *_: (0, 0)),
            pl.BlockSpec((1, DIFF_DV), lambda *_: (0, 0)),
        ],
        out_specs=pl.BlockSpec((None, s, LANES), lambda bi, h: (bi, 0, h)),
        out_shape=jax.ShapeDtypeStruct((b, s, DIFF_WIDTH), BF16),
        scratch_shapes=[pltpu.VMEM((DIFF_DV, s), BF16)],
        compiler_params=params(dimension_semantics=("arbitrary", "arbitrary")),
        name="diffattn",
    )(z3, z3, z3, z3, diag_bias, vec(diff_lq1), vec(diff_lk1), vec(diff_lq2), vec(diff_lk2), vec(diff_subln))

    resident = functools.partial(pl.BlockSpec, pipeline_mode=pl.Buffered(1))
    out = pl.pallas_call(
        _out_kernel,
        grid=(m // OUT_TM,),
        in_specs=[
            pl.BlockSpec((OUT_TM, d), lambda i: (i, 0)),
            pl.BlockSpec((OUT_TM, RET_WIDTH), lambda i: (i, 0)),
            pl.BlockSpec((OUT_TM, DIFF_WIDTH), lambda i: (i, 0)),
            pl.BlockSpec((OUT_TM, PLE_DIM), lambda i: (i, 0)),
            resident((d, d), lambda i: (0, 0)),
            pl.BlockSpec((1, d), lambda i: (0, 0)),
            resident((d, d), lambda i: (0, 0)),
            resident((PLE_DIM, d), lambda i: (0, 0)),
        ],
        out_specs=pl.BlockSpec((OUT_TM, d), lambda i: (i, 0)),
        out_shape=jax.ShapeDtypeStruct((m, d), F32),
        compiler_params=params(dimension_semantics=("arbitrary",)),
        name="outproj",
    )(x2, ro.reshape(m, RET_WIDTH), do.reshape(m, DIFF_WIDTH), p2, w_out[0].astype(BF16),
      ple_norm[0].reshape(1, d), w_ple_gate[0].astype(BF16), w_ple_proj[0].astype(BF16))

    return out.reshape(b, s, d)
```

```python
import functools
import math

import jax
import jax.numpy as jnp
from jax import lax
from jax.experimental import pallas as pl
from jax.experimental.pallas import tpu as pltpu

F32 = jnp.float32
BF16 = jnp.bfloat16

D_MODEL = 2048
CHUNK = 64
PLE_DIM = 256
RET_WIDTH = 1024
DIFF_WIDTH = 1024
RET_HEADS = 8
RET_DV = 128
RET_DK = 64
DIFF_HEADS = 8
DIFF_DV = 128
DIFF_DH = 64
ROPE_BASE = 10000.0
EPS = 1e-6
D_IN = 7168
LAM_INIT = 0.8 - 0.6 * math.exp(-0.3 * 0)

LANES = 128
VMEM_LIMIT_BYTES = 56 * 1024 * 1024

RQ_BLK = 0
RK_BLK = 512 // LANES
DQ_BLK = 3072 // LANES
DK_BLK = 4096 // LANES
DV_BLK = 5120 // LANES
DG_BLK = 6144 // LANES

PROJ_TM = 1024
PROJ_TN = 1024
ROW_CHUNK = 256
RET_T = 256
ATT_TQ = 256
ATT_TK = 256
OUT_TM = 256
LOG2E = math.log2(math.e)
NEG_BIG = -1e30


def _silu(a):
    return a * (1.0 / (1.0 + jnp.exp(-a)))


def _swap_halves(a):
    lane = lax.broadcasted_iota(jnp.int32, a.shape, 1)
    first = (lane % RET_DK) < (RET_DK // 2)
    return jnp.where(first, pltpu.roll(a, LANES - RET_DK // 2, 1), pltpu.roll(a, RET_DK // 2, 1))


def _proj_kernel(x_ref, g_ref, w_ref, cos_ref, sin_ref, qn_ref, kn_ref, o_ref, u_ref):
    j = pl.program_id(1)
    tm = x_ref.shape[0]
    tn = w_ref.shape[1]

    def rope_epilogue(a, rows):
        c = cos_ref[rows, :]
        s = sin_ref[rows, :]
        for sl in range(tn // LANES):
            scale = 1.0 if sl * LANES < RET_HEADS * RET_DK else RET_DK ** -0.5
            asl = a[:, sl * LANES:(sl + 1) * LANES]
            o_ref[rows, pl.ds(sl * LANES, LANES)] = ((asl * c + _swap_halves(asl) * s) * scale).astype(BF16)

    def norm_epilogue(gn_ref, scale):
        def apply(a, rows):
            low = lax.broadcasted_iota(jnp.int32, (1, LANES), 1) < DIFF_DH
            for sl in range(tn // LANES):
                asl = a[:, sl * LANES:(sl + 1) * LANES]
                sq = asl * asl
                s_lo = jnp.sum(jnp.where(low, sq, 0.0), axis=-1, keepdims=True)
                s_hi = jnp.sum(jnp.where(low, 0.0, sq), axis=-1, keepdims=True)
                ms = jnp.where(low, s_lo, s_hi) * (1.0 / DIFF_DH)
                o_ref[rows, pl.ds(sl * LANES, LANES)] = (asl * lax.rsqrt(ms + EPS) * gn_ref[...] * scale).astype(BF16)
        return apply

    def silu_epilogue(a, rows):
        o_ref[rows, :] = _silu(a).astype(BF16)

    def plain_epilogue(a, rows):
        o_ref[rows, :] = a.astype(BF16)

    def run(epilogue, normalize_input=False):
        sizes = [ROW_CHUNK] * (tm // ROW_CHUNK - 1) + [ROW_CHUNK // 2] * 2
        for r, size in enumerate(sizes):
            rows = pl.ds(sum(sizes[:r]), size)
            if normalize_input:
                x = x_ref[rows, :]
                ms = jnp.mean(x * x, axis=-1, keepdims=True)
                u_ref[rows, :] = (x * lax.rsqrt(ms + EPS) * g_ref[...]).astype(BF16)
            epilogue(jnp.dot(u_ref[rows, :], w_ref[...], preferred_element_type=F32), rows)

    @pl.when(j == 0)
    def _():
        run(rope_epilogue, normalize_input=True)

    @pl.when((j == 1) | (j == 5))
    def _():
        run(plain_epilogue)

    @pl.when((j == 2) | (j == 6))
    def _():
        run(silu_epilogue)

    @pl.when(j == 3)
    def _():
        run(norm_epilogue(qn_ref, DIFF_DH ** -0.5 * LOG2E))

    @pl.when(j == 4)
    def _():
        run(norm_epilogue(kn_ref, 1.0))


def _retention_kernel(q_ref, k_ref, v_ref, g_ref, gn_ref, dmat_ref, xi_ref, zeta_ref, gdec_ref, o_ref, state_ref):
    s_len = q_ref.shape[0]
    lane = lax.broadcasted_iota(jnp.int32, (1, LANES), 1)
    state_ref[...] = jnp.zeros_like(state_ref)
    for n in range(s_len // RET_T):
        rows = pl.ds(n * RET_T, RET_T)
        q = q_ref[rows, :]
        k = k_ref[rows, :]
        v = v_ref[rows, :]
        cross = jnp.dot(q, state_ref[...].astype(BF16), preferred_element_type=F32)
        vz = (v.astype(F32) * zeta_ref[...]).astype(BF16)
        kv = lax.dot_general(k, vz, (((0,), (0,)), ((), ())), preferred_element_type=F32)
        state_ref[...] = state_ref[...] * gdec_ref[0] + kv * gdec_ref[1]
        for hh in range(2):
            cols = pl.ds(hh * RET_DV, RET_DV)
            qm = jnp.where((lane < RET_DK) == (hh == 0), q, jnp.zeros_like(q))
            sc = lax.dot_general(qm, k, (((1,), (1,)), ((), ())), preferred_element_type=F32)
            sc = sc * dmat_ref[hh]
            intra = jnp.dot(sc.astype(BF16), v[:, hh * RET_DV:(hh + 1) * RET_DV], preferred_element_type=F32)
            o = intra + cross[:, hh * RET_DV:(hh + 1) * RET_DV] * xi_ref[:, cols]
            ms = jnp.mean(o * o, axis=-1, keepdims=True)
            o = o * lax.rsqrt(ms + EPS) * gn_ref[:, cols]
            o_ref[rows, cols] = (o * g_ref[rows, cols].astype(F32)).astype(BF16)


def _diffattn_kernel(q_ref, k_ref, v_ref, g_ref, bias_ref, lq1_ref, lk1_ref, lq2_ref, lk2_ref, sub_ref, o_ref,
                     vt_ref):
    s_len = k_ref.shape[0]
    tq = ATT_TQ
    for t in range(s_len // ATT_TK):
        cols = pl.ds(t * ATT_TK, ATT_TK)
        vt_ref[:, cols] = v_ref[cols, :].astype(F32).T.astype(BF16)

    lam = (jnp.exp(jnp.sum(lq1_ref[...] * lk1_ref[...], axis=-1, keepdims=True))
           - jnp.exp(jnp.sum(lq2_ref[...] * lk2_ref[...], axis=-1, keepdims=True)) + LAM_INIT)
    for qi in range(s_len // tq):
        rows = pl.ds(qi * tq, tq)
        lo = qi * tq
        qt = q_ref[rows, :].astype(F32).T
        row = lax.broadcasted_iota(jnp.int32, qt.shape, 0)
        qbd = jnp.concatenate([jnp.where(row < DIFF_DH, qt, 0.0), jnp.where(row >= DIFF_DH, qt, 0.0)],
                              axis=1).astype(BF16)
        sd = jnp.dot(k_ref[lo:lo + tq, :], qbd, preferred_element_type=F32) + bias_ref[...]
        m = jnp.max(sd, axis=0, keepdims=True)
        if qi > 0:
            so = jnp.dot(k_ref[0:lo, :], qbd, preferred_element_type=F32)
            m = jnp.maximum(m, jnp.max(so, axis=0, keepdims=True))
        pd = jnp.exp2(sd - m)
        l = jnp.sum(pd, axis=0, keepdims=True)
        p = pd.astype(BF16)
        if qi > 0:
            po = jnp.exp2(so - m)
            l = l + jnp.sum(po, axis=0, keepdims=True)
            p = jnp.concatenate([po.astype(BF16), p], axis=0)
        on = jnp.dot(vt_ref[:, 0:lo + tq], p, preferred_element_type=F32) / l
        o = (on[:, 0:tq] - lam * on[:, tq:2 * tq]).T
        ms_o = jnp.mean(o * o, axis=-1, keepdims=True)
        o = o * lax.rsqrt(ms_o + EPS) * sub_ref[...] * (1.0 - LAM_INIT)
        o_ref[rows, :] = (o * g_ref[rows, :].astype(F32)).astype(BF16)


def _out_kernel(x_ref, ro_ref, do_ref, p_ref, wo_ref, pn_ref, wg_ref, wp_ref, o_ref):
    acc = jnp.dot(ro_ref[...], wo_ref[0:RET_WIDTH, :], preferred_element_type=F32)
    acc = acc + jnp.dot(do_ref[...], wo_ref[RET_WIDTH:, :], preferred_element_type=F32)
    h1 = x_ref[...] + acc
    ms = jnp.mean(h1 * h1, axis=-1, keepdims=True)
    hn = (h1 * lax.rsqrt(ms + EPS) * pn_ref[...]).astype(BF16)
    z = jnp.dot(hn, wg_ref[...], preferred_element_type=F32)
    gate = 1.0 / (1.0 + jnp.exp(-z))
    ple = jnp.dot(p_ref[...].astype(BF16), wp_ref[...], preferred_element_type=F32)
    o_ref[...] = h1 + gate * ple


def _decay_tables():
    log_g = jnp.log1p(-jnp.exp2(-5.0 - jnp.arange(RET_HEADS, dtype=F32)))
    idx = jnp.arange(RET_T, dtype=F32)
    dist = jnp.abs(idx[:, None] - idx[None, :])
    same_or_earlier = (jnp.arange(RET_T)[None, :] // CHUNK) <= (jnp.arange(RET_T)[:, None] // CHUNK)
    dmat = jnp.where(same_or_earlier[None], jnp.exp(dist[None] * log_g[:, None, None]), 0.0)
    xi = jnp.exp((idx + 1.0)[None, :] * log_g[:, None])
    zeta = jnp.exp((RET_T - 1.0 - idx)[None, :] * log_g[:, None])
    g_step = jnp.exp(RET_T * log_g)
    n_pair = RET_HEADS // 2
    widen = lambda t: jnp.broadcast_to(t.reshape(n_pair, 2, RET_T, 1), (n_pair, 2, RET_T, RET_DV)) \
        .transpose(0, 2, 1, 3).reshape(n_pair, RET_T, 2 * RET_DV)
    xi_w = widen(xi)
    zeta_w = widen(zeta)
    rowh = (jnp.arange(2 * RET_DK) // RET_DK)[:, None]
    colh = (jnp.arange(2 * RET_DV) // RET_DV)[None, :]
    diag = (rowh == colh).astype(F32)
    gp = g_step.reshape(n_pair, 2)
    gq = jnp.where(colh[None] == 0, gp[:, 0, None, None], gp[:, 1, None, None]) * diag[None]
    gdec = jnp.stack([gq, jnp.broadcast_to(diag[None], gq.shape)], axis=1)
    return dmat, xi_w, zeta_w, gdec


def _rope_tables(s_len):
    pos = jnp.arange(s_len, dtype=F32)
    inv_freq = ROPE_BASE ** (-jnp.arange(RET_DK // 2, dtype=F32) / (RET_DK // 2))
    ang = pos[:, None] * inv_freq[None, :]
    cos, sin = jnp.cos(ang), jnp.sin(ang)
    cos_t = jnp.tile(cos, (1, LANES // (RET_DK // 2)))
    sin_t = jnp.tile(jnp.concatenate([-sin, sin], axis=-1), (1, LANES // RET_DK))
    return cos_t, sin_t


def kernel(x, p, attn_norm, w_in, ret_gn, diff_qn, diff_kn, diff_lq1, diff_lk1, diff_lq2, diff_lk2, diff_subln, w_out, ple_norm, w_ple_gate, w_ple_proj):
    b, s, d = x.shape
    m = b * s
    assert d == D_MODEL and s % PROJ_TM == 0 and m % OUT_TM == 0 and s % RET_T == 0 and s % ATT_TQ == 0
    x2 = x.reshape(m, d)
    p2 = p[0].reshape(m, PLE_DIM)

    cos_t, sin_t = _rope_tables(s)
    dmat, xi_w, zeta_w, gdec = _decay_tables()
    qn_t = jnp.tile(diff_qn[0], LANES // DIFF_DH).reshape(1, LANES)
    kn_t = jnp.tile(diff_kn[0], LANES // DIFF_DH).reshape(1, LANES)

    params = functools.partial(pltpu.CompilerParams, vmem_limit_bytes=VMEM_LIMIT_BYTES)
    const = lambda *_: (0, 0)
    pos_blocks = s // PROJ_TM

    z = pl.pallas_call(
        _proj_kernel,
        grid=(m // PROJ_TM, D_IN // PROJ_TN),
        in_specs=[
            pl.BlockSpec((PROJ_TM, d), lambda i, j: (i, 0)),
            pl.BlockSpec((1, d), const),
            pl.BlockSpec((d, PROJ_TN), lambda i, j: (0, j)),
            pl.BlockSpec((PROJ_TM, LANES), lambda i, j: (i % pos_blocks, 0)),
            pl.BlockSpec((PROJ_TM, LANES), lambda i, j: (i % pos_blocks, 0)),
            pl.BlockSpec((1, LANES), const),
            pl.BlockSpec((1, LANES), const),
        ],
        out_specs=pl.BlockSpec((PROJ_TM, PROJ_TN), lambda i, j: (i, j)),
        out_shape=jax.ShapeDtypeStruct((m, D_IN), BF16),
        scratch_shapes=[pltpu.VMEM((PROJ_TM, d), BF16)],
        compiler_params=params(dimension_semantics=("arbitrary", "arbitrary")),
        name="proj",
    )(x2, attn_norm[0].reshape(1, d), w_in[0].astype(BF16), cos_t, sin_t, qn_t, kn_t)

    z3 = z.reshape(b, s, D_IN)
    n_pair = RET_HEADS // 2
    ro = pl.pallas_call(
        _retention_kernel,
        grid=(b, n_pair),
        in_specs=[
            pl.BlockSpec((None, s, LANES), lambda bi, hp: (bi, 0, RQ_BLK + hp)),
            pl.BlockSpec((None, s, LANES), lambda bi, hp: (bi, 0, RK_BLK + hp)),
            pl.BlockSpec((None, s, 2 * RET_DV), lambda bi, hp: (bi, 0, 1024 // 256 + hp)),
            pl.BlockSpec((None, s, 2 * RET_DV), lambda bi, hp: (bi, 0, 2048 // 256 + hp)),
            pl.BlockSpec((1, 2 * RET_DV), lambda bi, hp: (0, hp)),
            pl.BlockSpec((2, RET_T, RET_T), lambda bi, hp: (hp, 0, 0)),
            pl.BlockSpec((None, RET_T, 2 * RET_DV), lambda bi, hp: (hp, 0, 0)),
            pl.BlockSpec((None, RET_T, 2 * RET_DV), lambda bi, hp: (hp, 0, 0)),
            pl.BlockSpec((None, 2, 2 * RET_DK, 2 * RET_DV), lambda bi, hp: (hp, 0, 0, 0)),
        ],
        out_specs=pl.BlockSpec((None, s, 2 * RET_DV), lambda bi, hp: (bi, 0, hp)),
        out_shape=jax.ShapeDtypeStruct((b, s, RET_WIDTH), BF16),
        scratch_shapes=[pltpu.VMEM((2 * RET_DK, 2 * RET_DV), F32)],
        compiler_params=params(dimension_semantics=("arbitrary", "arbitrary")),
        name="retention",
    )(z3, z3, z3, z3, ret_gn[0].reshape(1, RET_WIDTH), dmat, xi_w, zeta_w, gdec)

    vec = lambda a: a[0].reshape(1, -1)
    tile_chunk = jnp.arange(ATT_TK) // CHUNK
    diag_bias = jnp.where(tile_chunk[:, None] <= tile_chunk[None, :], 0.0, NEG_BIG).astype(F32)
    diag_bias = jnp.tile(diag_bias, (1, 2))
    do = pl.pallas_call(
        _diffattn_kernel,
        grid=(b, DIFF_HEADS),
        in_specs=[
            pl.BlockSpec((None, s, LANES), lambda bi, h: (bi, 0, DQ_BLK + h)),
            pl.BlockSpec((None, s, LANES), lambda bi, h: (bi, 0, DK_BLK + h)),
            pl.BlockSpec((None, s, LANES), lambda bi, h: (bi, 0, DV_BLK + h)),
            pl.BlockSpec((None, s, LANES), lambda bi, h: (bi, 0, DG_BLK + h)),
            pl.BlockSpec((ATT_TK, 2 * ATT_TQ), lambda *_: (0, 0)),
            pl.BlockSpec((1, DIFF_DH), lambda *_: (0, 0)),
            pl.BlockSpec((1, DIFF_DH), lambda *_: (0, 0)),
            pl.BlockSpec((1, DIFF_DH), lambda *_: (0, 0)),
            pl.BlockSpec((1, DIFF_DH), lambda *_: (0, 0)),
            pl.BlockSpec((1, DIFF_DV), lambda *_: (0, 0)),
        ],
        out_specs=pl.BlockSpec((None, s, LANES), lambda bi, h: (bi, 0, h)),
        out_shape=jax.ShapeDtypeStruct((b, s, DIFF_WIDTH), BF16),
        scratch_shapes=[pltpu.VMEM((DIFF_DV, s), BF16)],
        compiler_params=params(dimension_semantics=("arbitrary", "arbitrary")),
        name="diffattn",
    )(z3, z3, z3, z3, diag_bias, vec(diff_lq1), vec(diff_lk1), vec(diff_lq2), vec(diff_lk2), vec(diff_subln))

    resident = functools.partial(pl.BlockSpec, pipeline_mode=pl.Buffered(1))
    out = pl.pallas_call(
        _out_kernel,
        grid=(m // OUT_TM,),
        in_specs=[
            pl.BlockSpec((OUT_TM, d), lambda i: (i, 0)),
            pl.BlockSpec((OUT_TM, RET_WIDTH), lambda i: (i, 0)),
            pl.BlockSpec((OUT_TM, DIFF_WIDTH), lambda i: (i, 0)),
            pl.BlockSpec((OUT_TM, PLE_DIM), lambda i: (i, 0)),
            resident((d, d), lambda i: (0, 0)),
            pl.BlockSpec((1, d), lambda i: (0, 0)),
            resident((d, d), lambda i: (0, 0)),
            resident((PLE_DIM, d), lambda i: (0, 0)),
        ],
        out_specs=pl.BlockSpec((OUT_TM, d), lambda i: (i, 0)),
        out_shape=jax.ShapeDtypeStruct((m, d), F32),
        compiler_params=params(dimension_semantics=("arbitrary",)),
        name="outproj",
    )(x2, ro.reshape(m, RET_WIDTH), do.reshape(m, DIFF_WIDTH), p2, w_out[0].astype(BF16),
      ple_norm[0].reshape(1, d), w_ple_gate[0].astype(BF16), w_ple_proj[0].astype(BF16))

    return out.reshape(b, s, d)
```

```python
import functools
import math

import jax
import jax.numpy as jnp
from jax import lax
from jax.experimental import pallas as pl
from jax.experimental.pallas import tpu as pltpu

F32 = jnp.float32
BF16 = jnp.bfloat16

D_MODEL = 2048
CHUNK = 64
PLE_DIM = 256
RET_WIDTH = 1024
DIFF_WIDTH = 1024
RET_HEADS = 8
RET_DV = 128
RET_DK = 64
DIFF_HEADS = 8
DIFF_DV = 128
DIFF_DH = 64
ROPE_BASE = 10000.0
EPS = 1e-6
D_IN = 7168
LAM_INIT = 0.8 - 0.6 * math.exp(-0.3 * 0)

LANES = 128
VMEM_LIMIT_BYTES = 56 * 1024 * 1024

RQ_BLK = 0
RK_BLK = 512 // LANES
DQ_BLK = 3072 // LANES
DK_BLK = 4096 // LANES
DV_BLK = 5120 // LANES
DG_BLK = 6144 // LANES

PROJ_TM = 1024
PROJ_TN = 1024
ROW_CHUNK = 256
RET_T = 256
ATT_TQ = 256
ATT_TK = 256
OUT_TM = 256
LOG2E = math.log2(math.e)
NEG_BIG = -1e30


def _silu(a):
    return a * (1.0 / (1.0 + jnp.exp(-a)))


def _swap_halves(a):
    lane = lax.broadcasted_iota(jnp.int32, a.shape, 1)
    first = (lane % RET_DK) < (RET_DK // 2)
    return jnp.where(first, pltpu.roll(a, LANES - RET_DK // 2, 1), pltpu.roll(a, RET_DK // 2, 1))


def _proj_kernel(x_ref, g_ref, w_ref, cos_ref, sin_ref, qn_ref, kn_ref, o_ref, u_ref):
    j = pl.program_id(1)
    tm = x_ref.shape[0]
    tn = w_ref.shape[1]

    def rope_epilogue(a, rows):
        c = cos_ref[rows, :]
        s = sin_ref[rows, :]
        for sl in range(tn // LANES):
            scale = 1.0 if sl * LANES < RET_HEADS * RET_DK else RET_DK ** -0.5
            asl = a[:, sl * LANES:(sl + 1) * LANES]
            o_ref[rows, pl.ds(sl * LANES, LANES)] = ((asl * c + _swap_halves(asl) * s) * scale).astype(BF16)

    def norm_epilogue(gn_ref, scale):
        def apply(a, rows):
            low = lax.broadcasted_iota(jnp.int32, (1, LANES), 1) < DIFF_DH
            for sl in range(tn // LANES):
                asl = a[:, sl * LANES:(sl + 1) * LANES]
                sq = asl * asl
                s_lo = jnp.sum(jnp.where(low, sq, 0.0), axis=-1, keepdims=True)
                s_hi = jnp.sum(jnp.where(low, 0.0, sq), axis=-1, keepdims=True)
                ms = jnp.where(low, s_lo, s_hi) * (1.0 / DIFF_DH)
                o_ref[rows, pl.ds(sl * LANES, LANES)] = (asl * lax.rsqrt(ms + EPS) * gn_ref[...] * scale).astype(BF16)
        return apply

    def silu_epilogue(a, rows):
        o_ref[rows, :] = _silu(a).astype(BF16)

    def plain_epilogue(a, rows):
        o_ref[rows, :] = a.astype(BF16)

    def run(epilogue, normalize_input=False):
        sizes = [ROW_CHUNK] * (tm // ROW_CHUNK - 1) + [ROW_CHUNK // 2] * 2
        for r, size in enumerate(sizes):
            rows = pl.ds(sum(sizes[:r]), size)
            if normalize_input:
                x = x_ref[rows, :]
                ms = jnp.mean(x * x, axis=-1, keepdims=True)
                u_ref[rows, :] = (x * lax.rsqrt(ms + EPS) * g_ref[...]).astype(BF16)
            epilogue(jnp.dot(u_ref[rows, :], w_ref[...], preferred_element_type=F32), rows)

    @pl.when(j == 0)
    def _():
        run(rope_epilogue, normalize_input=True)

    @pl.when((j == 1) | (j == 5))
    def _():
        run(plain_epilogue)

    @pl.when((j == 2) | (j == 6))
    def _():
        run(silu_epilogue)

    @pl.when(j == 3)
    def _():
        run(norm_epilogue(qn_ref, DIFF_DH ** -0.5 * LOG2E))

    @pl.when(j == 4)
    def _():
        run(norm_epilogue(kn_ref, 1.0))


def _retention_kernel(q_ref, k_ref, v_ref, g_ref, gn_ref, dmat_ref, xi_ref, zeta_ref, gdec_ref, o_ref, state_ref):
    s_len = q_ref.shape[0]
    lane = lax.broadcasted_iota(jnp.int32, (1, LANES), 1)
    state_ref[...] = jnp.zeros_like(state_ref)
    for n in range(s_len // RET_T):
        rows = pl.ds(n * RET_T, RET_T)
        q = q_ref[rows, :]
        k = k_ref[rows, :]
        v = v_ref[rows, :]
        cross = jnp.dot(q, state_ref[...].astype(BF16), preferred_element_type=F32)
        vz = (v.astype(F32) * zeta_ref[...]).astype(BF16)
        kv = lax.dot_general(k, vz, (((0,), (0,)), ((), ())), preferred_element_type=F32)
        state_ref[...] = state_ref[...] * gdec_ref[0] + kv * gdec_ref[1]
        for hh in range(2):
            cols = pl.ds(hh * RET_DV, RET_DV)
            qm = jnp.where((lane < RET_DK) == (hh == 0), q, jnp.zeros_like(q))
            sc = lax.dot_general(qm, k, (((1,), (1,)), ((), ())), preferred_element_type=F32)
            sc = sc * dmat_ref[hh]
            intra = jnp.dot(sc.astype(BF16), v[:, hh * RET_DV:(hh + 1) * RET_DV], preferred_element_type=F32)
            o = intra + cross[:, hh * RET_DV:(hh + 1) * RET_DV] * xi_ref[:, cols]
            ms = jnp.mean(o * o, axis=-1, keepdims=True)
            o = o * lax.rsqrt(ms + EPS) * gn_ref[:, cols]
            o_ref[rows, cols] = (o * g_ref[rows, cols].astype(F32)).astype(BF16)


def _diffattn_kernel(q_ref, k_ref, v_ref, g_ref, bias_ref, lq1_ref, lk1_ref, lq2_ref, lk2_ref, sub_ref, o_ref,
                     vt_ref):
    s_len = k_ref.shape[0]
    tq = ATT_TQ
    for t in range(s_len // ATT_TK):
        cols = pl.ds(t * ATT_TK, ATT_TK)
        vt_ref[:, cols] = v_ref[cols, :].astype(F32).T.astype(BF16)

    lam = (jnp.exp(jnp.sum(lq1_ref[...] * lk1_ref[...], axis=-1, keepdims=True))
           - jnp.exp(jnp.sum(lq2_ref[...] * lk2_ref[...], axis=-1, keepdims=True)) + LAM_INIT)
    def scores(qi):
        lo = qi * tq
        qt = q_ref[pl.ds(lo, tq), :].astype(F32).T
        row = lax.broadcasted_iota(jnp.int32, qt.shape, 0)
        qbd = jnp.concatenate([jnp.where(row < DIFF_DH, qt, 0.0), jnp.where(row >= DIFF_DH, qt, 0.0)],
                              axis=1).astype(BF16)
        sd = jnp.dot(k_ref[lo:lo + tq, :], qbd, preferred_element_type=F32) + bias_ref[...]
        m = jnp.max(sd, axis=0, keepdims=True)
        so = None
        if qi > 0:
            so = jnp.dot(k_ref[0:lo, :], qbd, preferred_element_type=F32)
            m = jnp.maximum(m, jnp.max(so, axis=0, keepdims=True))
        return sd, so, m

    def finish(qi, sd, so, m):
        lo = qi * tq
        rows = pl.ds(lo, tq)
        pd = jnp.exp2(sd - m)
        l = jnp.sum(pd, axis=0, keepdims=True)
        p = pd.astype(BF16)
        if qi > 0:
            po = jnp.exp2(so - m)
            l = l + jnp.sum(po, axis=0, keepdims=True)
            p = jnp.concatenate([po.astype(BF16), p], axis=0)
        on = jnp.dot(vt_ref[:, 0:lo + tq], p, preferred_element_type=F32) / l
        o = (on[:, 0:tq] - lam * on[:, tq:2 * tq]).T
        ms_o = jnp.mean(o * o, axis=-1, keepdims=True)
        o = o * lax.rsqrt(ms_o + EPS) * sub_ref[...] * (1.0 - LAM_INIT)
        o_ref[rows, :] = (o * g_ref[rows, :].astype(F32)).astype(BF16)

    order = list(range(s_len // tq))[::-1]
    cur = scores(order[0])
    for n, qi in enumerate(order):
        nxt = scores(order[n + 1]) if n + 1 < len(order) else None
        finish(qi, *cur)
        cur = nxt


def _out_kernel(x_ref, ro_ref, do_ref, p_ref, wo_ref, pn_ref, wg_ref, wp_ref, o_ref):
    acc = jnp.dot(ro_ref[...], wo_ref[0:RET_WIDTH, :], preferred_element_type=F32)
    acc = acc + jnp.dot(do_ref[...], wo_ref[RET_WIDTH:, :], preferred_element_type=F32)
    h1 = x_ref[...] + acc
    ms = jnp.mean(h1 * h1, axis=-1, keepdims=True)
    hn = (h1 * lax.rsqrt(ms + EPS) * pn_ref[...]).astype(BF16)
    z = jnp.dot(hn, wg_ref[...], preferred_element_type=F32)
    gate = 1.0 / (1.0 + jnp.exp(-z))
    ple = jnp.dot(p_ref[...].astype(BF16), wp_ref[...], preferred_element_type=F32)
    o_ref[...] = h1 + gate * ple


def _decay_tables():
    log_g = jnp.log1p(-jnp.exp2(-5.0 - jnp.arange(RET_HEADS, dtype=F32)))
    idx = jnp.arange(RET_T, dtype=F32)
    dist = jnp.abs(idx[:, None] - idx[None, :])
    same_or_earlier = (jnp.arange(RET_T)[None, :] // CHUNK) <= (jnp.arange(RET_T)[:, None] // CHUNK)
    dmat = jnp.where(same_or_earlier[None], jnp.exp(dist[None] * log_g[:, None, None]), 0.0)
    xi = jnp.exp((idx + 1.0)[None, :] * log_g[:, None])
    zeta = jnp.exp((RET_T - 1.0 - idx)[None, :] * log_g[:, None])
    g_step = jnp.exp(RET_T * log_g)
    n_pair = RET_HEADS // 2
    widen = lambda t: jnp.broadcast_to(t.reshape(n_pair, 2, RET_T, 1), (n_pair, 2, RET_T, RET_DV)) \
        .transpose(0, 2, 1, 3).reshape(n_pair, RET_T, 2 * RET_DV)
    xi_w = widen(xi)
    zeta_w = widen(zeta)
    rowh = (jnp.arange(2 * RET_DK) // RET_DK)[:, None]
    colh = (jnp.arange(2 * RET_DV) // RET_DV)[None, :]
    diag = (rowh == colh).astype(F32)
    gp = g_step.reshape(n_pair, 2)
    gq = jnp.where(colh[None] == 0, gp[:, 0, None, None], gp[:, 1, None, None]) * diag[None]
    gdec = jnp.stack([gq, jnp.broadcast_to(diag[None], gq.shape)], axis=1)
    return dmat, xi_w, zeta_w, gdec


def _rope_tables(s_len):
    pos = jnp.arange(s_len, dtype=F32)
    inv_freq = ROPE_BASE ** (-jnp.arange(RET_DK // 2, dtype=F32) / (RET_DK // 2))
    ang = pos[:, None] * inv_freq[None, :]
    cos, sin = jnp.cos(ang), jnp.sin(ang)
    cos_t = jnp.tile(cos, (1, LANES // (RET_DK // 2)))
    sin_t = jnp.tile(jnp.concatenate([-sin, sin], axis=-1), (1, LANES // RET_DK))
    return cos_t, sin_t


def kernel(x, p, attn_norm, w_in, ret_gn, diff_qn, diff_kn, diff_lq1, diff_lk1, diff_lq2, diff_lk2, diff_subln, w_out, ple_norm, w_ple_gate, w_ple_proj):
    b, s, d = x.shape
    m = b * s
    assert d == D_MODEL and s % PROJ_TM == 0 and m % OUT_TM == 0 and s % RET_T == 0 and s % ATT_TQ == 0
    x2 = x.reshape(m, d)
    p2 = p[0].reshape(m, PLE_DIM)

    cos_t, sin_t = _rope_tables(s)
    dmat, xi_w, zeta_w, gdec = _decay_tables()
    qn_t = jnp.tile(diff_qn[0], LANES // DIFF_DH).reshape(1, LANES)
    kn_t = jnp.tile(diff_kn[0], LANES // DIFF_DH).reshape(1, LANES)

    params = functools.partial(pltpu.CompilerParams, vmem_limit_bytes=VMEM_LIMIT_BYTES)
    const = lambda *_: (0, 0)
    pos_blocks = s // PROJ_TM

    z = pl.pallas_call(
        _proj_kernel,
        grid=(m // PROJ_TM, D_IN // PROJ_TN),
        in_specs=[
            pl.BlockSpec((PROJ_TM, d), lambda i, j: (i, 0)),
            pl.BlockSpec((1, d), const),
            pl.BlockSpec((d, PROJ_TN), lambda i, j: (0, j)),
            pl.BlockSpec((PROJ_TM, LANES), lambda i, j: (i % pos_blocks, 0)),
            pl.BlockSpec((PROJ_TM, LANES), lambda i, j: (i % pos_blocks, 0)),
            pl.BlockSpec((1, LANES), const),
            pl.BlockSpec((1, LANES), const),
        ],
        out_specs=pl.BlockSpec((PROJ_TM, PROJ_TN), lambda i, j: (i, j)),
        out_shape=jax.ShapeDtypeStruct((m, D_IN), BF16),
        scratch_shapes=[pltpu.VMEM((PROJ_TM, d), BF16)],
        compiler_params=params(dimension_semantics=("arbitrary", "arbitrary")),
        name="proj",
    )(x2, attn_norm[0].reshape(1, d), w_in[0].astype(BF16), cos_t, sin_t, qn_t, kn_t)

    z3 = z.reshape(b, s, D_IN)
    n_pair = RET_HEADS // 2
    ro = pl.pallas_call(
        _retention_kernel,
        grid=(b, n_pair),
        in_specs=[
            pl.BlockSpec((None, s, LANES), lambda bi, hp: (bi, 0, RQ_BLK + hp)),
            pl.BlockSpec((None, s, LANES), lambda bi, hp: (bi, 0, RK_BLK + hp)),
            pl.BlockSpec((None, s, 2 * RET_DV), lambda bi, hp: (bi, 0, 1024 // 256 + hp)),
            pl.BlockSpec((None, s, 2 * RET_DV), lambda bi, hp: (bi, 0, 2048 // 256 + hp)),
            pl.BlockSpec((1, 2 * RET_DV), lambda bi, hp: (0, hp)),
            pl.BlockSpec((2, RET_T, RET_T), lambda bi, hp: (hp, 0, 0)),
            pl.BlockSpec((None, RET_T, 2 * RET_DV), lambda bi, hp: (hp, 0, 0)),
            pl.BlockSpec((None, RET_T, 2 * RET_DV), lambda bi, hp: (hp, 0, 0)),
            pl.BlockSpec((None, 2, 2 * RET_DK, 2 * RET_DV), lambda bi, hp: (hp, 0, 0, 0)),
        ],
        out_specs=pl.BlockSpec((None, s, 2 * RET_DV), lambda bi, hp: (bi, 0, hp)),
        out_shape=jax.ShapeDtypeStruct((b, s, RET_WIDTH), BF16),
        scratch_shapes=[pltpu.VMEM((2 * RET_DK, 2 * RET_DV), F32)],
        compiler_params=params(dimension_semantics=("arbitrary", "arbitrary")),
        name="retention",
    )(z3, z3, z3, z3, ret_gn[0].reshape(1, RET_WIDTH), dmat, xi_w, zeta_w, gdec)

    vec = lambda a: a[0].reshape(1, -1)
    tile_chunk = jnp.arange(ATT_TK) // CHUNK
    diag_bias = jnp.where(tile_chunk[:, None] <= tile_chunk[None, :], 0.0, NEG_BIG).astype(F32)
    diag_bias = jnp.tile(diag_bias, (1, 2))
    do = pl.pallas_call(
        _diffattn_kernel,
        grid=(b, DIFF_HEADS),
        in_specs=[
            pl.BlockSpec((None, s, LANES), lambda bi, h: (bi, 0, DQ_BLK + h)),
            pl.BlockSpec((None, s, LANES), lambda bi, h: (bi, 0, DK_BLK + h)),
            pl.BlockSpec((None, s, LANES), lambda bi, h: (bi, 0, DV_BLK + h)),
            pl.BlockSpec((None, s, LANES), lambda bi, h: (bi, 0, DG_BLK + h)),
            pl.BlockSpec((ATT_TK, 2 * ATT_TQ), lambda *_: (0, 0)),
            pl.BlockSpec((1, DIFF_DH), lambda *_: (0, 0)),
            pl.BlockSpec((1, DIFF_DH), lambda *_: (0, 0)),
            pl.BlockSpec((1, DIFF_DH), lambda *_: (0, 0)),
            pl.BlockSpec((1, DIFF_DH), lambda *_: (0, 0)),
            pl.BlockSpec((1, DIFF_DV), lambda *_: (0, 0)),
        ],
        out_specs=pl.BlockSpec((None, s, LANES), lambda bi, h: (bi, 0, h)),
        out_shape=jax.ShapeDtypeStruct((b, s, DIFF_WIDTH), BF16),
        scratch_shapes=[pltpu.VMEM((DIFF_DV, s), BF16)],
        compiler_params=params(dimension_semantics=("arbitrary", "arbitrary")),
        name="diffattn",
    )(z3, z3, z3, z3, diag_bias, vec(diff_lq1), vec(diff_lk1), vec(diff_lq2), vec(diff_lk2), vec(diff_subln))

    resident = functools.partial(pl.BlockSpec, pipeline_mode=pl.Buffered(1))
    out = pl.pallas_call(
        _out_kernel,
        grid=(m // OUT_TM,),
        in_specs=[
            pl.BlockSpec((OUT_TM, d), lambda i: (i, 0)),
            pl.BlockSpec((OUT_TM, RET_WIDTH), lambda i: (i, 0)),
            pl.BlockSpec((OUT_TM, DIFF_WIDTH), lambda i: (i, 0)),
            pl.BlockSpec((OUT_TM, PLE_DIM), lambda i: (i, 0)),
            resident((d, d), lambda i: (0, 0)),
            pl.BlockSpec((1, d), lambda i: (0, 0)),
            resident((d, d), lambda i: (0, 0)),
            resident((PLE_DIM, d), lambda i: (0, 0)),
        ],
        out_specs=pl.BlockSpec((OUT_TM, d), lambda i: (i, 0)),
        out_shape=jax.ShapeDtypeStruct((m, d), F32),
        compiler_params=params(dimension_semantics=("arbitrary",)),
        name="outproj",
    )(x2, ro.reshape(m, RET_WIDTH), do.reshape(m, DIFF_WIDTH), p2, w_out[0].astype(BF16),
      ple_norm[0].reshape(1, d), w_ple_gate[0].astype(BF16), w_ple_proj[0].astype(BF16))

    return out.reshape(b, s, d)
```

```python
import functools
import math

import jax
import jax.numpy as jnp
from jax import lax
from jax.experimental import pallas as pl
from jax.experimental.pallas import tpu as pltpu

F32 = jnp.float32
BF16 = jnp.bfloat16

D_MODEL = 2048
CHUNK = 64
PLE_DIM = 256
RET_WIDTH = 1024
DIFF_WIDTH = 1024
RET_HEADS = 8
RET_DV = 128
RET_DK = 64
DIFF_HEADS = 8
DIFF_DV = 128
DIFF_DH = 64
ROPE_BASE = 10000.0
EPS = 1e-6
D_IN = 7168
LAM_INIT = 0.8 - 0.6 * math.exp(-0.3 * 0)

LANES = 128
VMEM_LIMIT_BYTES = 56 * 1024 * 1024

RQ_BLK = 0
RK_BLK = 512 // LANES
DQ_BLK = 3072 // LANES
DK_BLK = 4096 // LANES
DV_BLK = 5120 // LANES
DG_BLK = 6144 // LANES

PROJ_TM = 1024
PROJ_TN = 1024
ROW_CHUNK = 256
RET_T = 256
ATT_TQ = 256
ATT_TK = 256
ATT_HEADS = 2
OUT_TM = 512
OUT_CHUNK = 256
LOG2E = math.log2(math.e)
NEG_BIG = -1e30


def _silu(a):
    return a * (1.0 / (1.0 + jnp.exp(-a)))


def _swap_halves(a):
    lane = lax.broadcasted_iota(jnp.int32, a.shape, 1)
    first = (lane % RET_DK) < (RET_DK // 2)
    return jnp.where(first, pltpu.roll(a, LANES - RET_DK // 2, 1), pltpu.roll(a, RET_DK // 2, 1))


def _proj_kernel(x_ref, g_ref, w_ref, cos_ref, sin_ref, qn_ref, kn_ref, o_ref, u_ref):
    j = pl.program_id(1)
    tm = x_ref.shape[0]
    tn = w_ref.shape[1]

    def rope_epilogue(a, rows):
        c = cos_ref[rows, :]
        s = sin_ref[rows, :]
        for sl in range(tn // LANES):
            scale = 1.0 if sl * LANES < RET_HEADS * RET_DK else RET_DK ** -0.5
            asl = a[:, sl * LANES:(sl + 1) * LANES]
            o_ref[rows, pl.ds(sl * LANES, LANES)] = ((asl * c + _swap_halves(asl) * s) * scale).astype(BF16)

    def norm_epilogue(gn_ref, scale):
        def apply(a, rows):
            low = lax.broadcasted_iota(jnp.int32, (1, LANES), 1) < DIFF_DH
            for sl in range(tn // LANES):
                asl = a[:, sl * LANES:(sl + 1) * LANES]
                sq = asl * asl
                s_lo = jnp.sum(jnp.where(low, sq, 0.0), axis=-1, keepdims=True)
                s_hi = jnp.sum(jnp.where(low, 0.0, sq), axis=-1, keepdims=True)
                ms = jnp.where(low, s_lo, s_hi) * (1.0 / DIFF_DH)
                o_ref[rows, pl.ds(sl * LANES, LANES)] = (asl * lax.rsqrt(ms + EPS) * gn_ref[...] * scale).astype(BF16)
        return apply

    def silu_epilogue(a, rows):
        o_ref[rows, :] = _silu(a).astype(BF16)

    def plain_epilogue(a, rows):
        o_ref[rows, :] = a.astype(BF16)

    def run(epilogue, normalize_input=False):
        sizes = [ROW_CHUNK] * (tm // ROW_CHUNK - 1) + [ROW_CHUNK // 2] * 2
        for r, size in enumerate(sizes):
            rows = pl.ds(sum(sizes[:r]), size)
            if normalize_input:
                x = x_ref[rows, :]
                ms = jnp.mean(x * x, axis=-1, keepdims=True)
                u_ref[rows, :] = (x * lax.rsqrt(ms + EPS) * g_ref[...]).astype(BF16)
            epilogue(jnp.dot(u_ref[rows, :], w_ref[...], preferred_element_type=F32), rows)

    @pl.when(j == 0)
    def _():
        run(rope_epilogue, normalize_input=True)

    @pl.when((j == 1) | (j == 5))
    def _():
        run(plain_epilogue)

    @pl.when((j == 2) | (j == 6))
    def _():
        run(silu_epilogue)

    @pl.when(j == 3)
    def _():
        run(norm_epilogue(qn_ref, DIFF_DH ** -0.5 * LOG2E))

    @pl.when(j == 4)
    def _():
        run(norm_epilogue(kn_ref, 1.0))


def _retention_kernel(q_ref, k_ref, v_ref, g_ref, gn_ref, dmat_ref, xi_ref, zeta_ref, gdec_ref, o_ref, state_ref):
    s_len = q_ref.shape[0]
    lane = lax.broadcasted_iota(jnp.int32, (1, LANES), 1)
    state_ref[...] = jnp.zeros_like(state_ref)
    for n in range(s_len // RET_T):
        rows = pl.ds(n * RET_T, RET_T)
        q = q_ref[rows, :]
        k = k_ref[rows, :]
        v = v_ref[rows, :]
        cross = jnp.dot(q, state_ref[...].astype(BF16), preferred_element_type=F32)
        vz = (v.astype(F32) * zeta_ref[...]).astype(BF16)
        kv = lax.dot_general(k, vz, (((0,), (0,)), ((), ())), preferred_element_type=F32)
        state_ref[...] = state_ref[...] * gdec_ref[0] + kv * gdec_ref[1]
        for hh in range(2):
            cols = pl.ds(hh * RET_DV, RET_DV)
            qm = jnp.where((lane < RET_DK) == (hh == 0), q, jnp.zeros_like(q))
            sc = lax.dot_general(qm, k, (((1,), (1,)), ((), ())), preferred_element_type=F32)
            sc = sc * dmat_ref[hh]
            intra = jnp.dot(sc.astype(BF16), v[:, hh * RET_DV:(hh + 1) * RET_DV], preferred_element_type=F32)
            o = intra + cross[:, hh * RET_DV:(hh + 1) * RET_DV] * xi_ref[:, cols]
            ms = jnp.mean(o * o, axis=-1, keepdims=True)
            o = o * lax.rsqrt(ms + EPS) * gn_ref[:, cols]
            o_ref[rows, cols] = (o * g_ref[rows, cols].astype(F32)).astype(BF16)


def _diffattn_kernel(q_ref, k_ref, v_ref, g_ref, bias_ref, lq1_ref, lk1_ref, lq2_ref, lk2_ref, sub_ref, o_ref,
                     vt_ref):
    s_len = k_ref.shape[0]
    tq = ATT_TQ
    for hh in range(ATT_HEADS):
        for t in range(s_len // ATT_TK):
            cols = pl.ds(t * ATT_TK, ATT_TK)
            vt_ref[hh, :, cols] = v_ref[cols, hh * DIFF_DV:(hh + 1) * DIFF_DV].astype(F32).T.astype(BF16)

    lam = (jnp.exp(jnp.sum(lq1_ref[...] * lk1_ref[...], axis=-1, keepdims=True))
           - jnp.exp(jnp.sum(lq2_ref[...] * lk2_ref[...], axis=-1, keepdims=True)) + LAM_INIT)

    def scores(hh, qi):
        lo = qi * tq
        hcols = slice(hh * LANES, (hh + 1) * LANES)
        qt = q_ref[lo:lo + tq, hcols].astype(F32).T
        row = lax.broadcasted_iota(jnp.int32, qt.shape, 0)
        qbd = jnp.concatenate([jnp.where(row < DIFF_DH, qt, 0.0), jnp.where(row >= DIFF_DH, qt, 0.0)],
                              axis=1).astype(BF16)
        sd = jnp.dot(k_ref[lo:lo + tq, hcols], qbd, preferred_element_type=F32) + bias_ref[...]
        m = jnp.max(sd, axis=0, keepdims=True)
        so = None
        if qi > 0:
            so = jnp.dot(k_ref[0:lo, hcols], qbd, preferred_element_type=F32)
            m = jnp.maximum(m, jnp.max(so, axis=0, keepdims=True))
        return sd, so, m

    def finish(hh, qi, sd, so, m):
        lo = qi * tq
        hcols = slice(hh * LANES, (hh + 1) * LANES)
        pd = jnp.exp2(sd - m)
        l = jnp.sum(pd, axis=0, keepdims=True)
        p = pd.astype(BF16)
        if qi > 0:
            po = jnp.exp2(so - m)
            l = l + jnp.sum(po, axis=0, keepdims=True)
            p = jnp.concatenate([po.astype(BF16), p], axis=0)
        on = jnp.dot(vt_ref[hh, :, 0:lo + tq], p, preferred_element_type=F32) / l
        o = (on[:, 0:tq] - lam * on[:, tq:2 * tq]).T
        ms_o = jnp.mean(o * o, axis=-1, keepdims=True)
        o = o * lax.rsqrt(ms_o + EPS) * sub_ref[...] * (1.0 - LAM_INIT)
        o_ref[lo:lo + tq, hcols] = (o * g_ref[lo:lo + tq, hcols].astype(F32)).astype(BF16)

    order = [(hh, qi) for hh in range(ATT_HEADS) for qi in reversed(range(s_len // tq))]
    cur = scores(*order[0])
    for n, unit in enumerate(order):
        nxt = scores(*order[n + 1]) if n + 1 < len(order) else None
        finish(*unit, *cur)
        cur = nxt


def _out_kernel(x_ref, ro_ref, do_ref, p_ref, wo_ref, pn_ref, wg_ref, wp_ref, o_ref):
    tm = x_ref.shape[0]

    def residual(rows):
        acc = jnp.dot(ro_ref[rows, :], wo_ref[0:RET_WIDTH, :], preferred_element_type=F32)
        acc = acc + jnp.dot(do_ref[rows, :], wo_ref[RET_WIDTH:, :], preferred_element_type=F32)
        h1 = x_ref[rows, :] + acc
        ms = jnp.mean(h1 * h1, axis=-1, keepdims=True)
        return h1, (h1 * lax.rsqrt(ms + EPS) * pn_ref[...]).astype(BF16)

    def gated(rows, h1, hn):
        z = jnp.dot(hn, wg_ref[...], preferred_element_type=F32)
        gate = 1.0 / (1.0 + jnp.exp(-z))
        ple = jnp.dot(p_ref[rows, :].astype(BF16), wp_ref[...], preferred_element_type=F32)
        o_ref[rows, :] = h1 + gate * ple

    chunks = [pl.ds(r * OUT_CHUNK, OUT_CHUNK) for r in range(tm // OUT_CHUNK)]
    cur = residual(chunks[0])
    for n, rows in enumerate(chunks):
        nxt = residual(chunks[n + 1]) if n + 1 < len(chunks) else None
        gated(rows, *cur)
        cur = nxt


def _decay_tables():
    log_g = jnp.log1p(-jnp.exp2(-5.0 - jnp.arange(RET_HEADS, dtype=F32)))
    idx = jnp.arange(RET_T, dtype=F32)
    dist = jnp.abs(idx[:, None] - idx[None, :])
    same_or_earlier = (jnp.arange(RET_T)[None, :] // CHUNK) <= (jnp.arange(RET_T)[:, None] // CHUNK)
    dmat = jnp.where(same_or_earlier[None], jnp.exp(dist[None] * log_g[:, None, None]), 0.0)
    xi = jnp.exp((idx + 1.0)[None, :] * log_g[:, None])
    zeta = jnp.exp((RET_T - 1.0 - idx)[None, :] * log_g[:, None])
    g_step = jnp.exp(RET_T * log_g)
    n_pair = RET_HEADS // 2
    widen = lambda t: jnp.broadcast_to(t.reshape(n_pair, 2, RET_T, 1), (n_pair, 2, RET_T, RET_DV)) \
        .transpose(0, 2, 1, 3).reshape(n_pair, RET_T, 2 * RET_DV)
    xi_w = widen(xi)
    zeta_w = widen(zeta)
    rowh = (jnp.arange(2 * RET_DK) // RET_DK)[:, None]
    colh = (jnp.arange(2 * RET_DV) // RET_DV)[None, :]
    diag = (rowh == colh).astype(F32)
    gp = g_step.reshape(n_pair, 2)
    gq = jnp.where(colh[None] == 0, gp[:, 0, None, None], gp[:, 1, None, None]) * diag[None]
    gdec = jnp.stack([gq, jnp.broadcast_to(diag[None], gq.shape)], axis=1)
    return dmat, xi_w, zeta_w, gdec


def _rope_tables(s_len):
    pos = jnp.arange(s_len, dtype=F32)
    inv_freq = ROPE_BASE ** (-jnp.arange(RET_DK // 2, dtype=F32) / (RET_DK // 2))
    ang = pos[:, None] * inv_freq[None, :]
    cos, sin = jnp.cos(ang), jnp.sin(ang)
    cos_t = jnp.tile(cos, (1, LANES // (RET_DK // 2)))
    sin_t = jnp.tile(jnp.concatenate([-sin, sin], axis=-1), (1, LANES // RET_DK))
    return cos_t, sin_t


def kernel(x, p, attn_norm, w_in, ret_gn, diff_qn, diff_kn, diff_lq1, diff_lk1, diff_lq2, diff_lk2, diff_subln, w_out, ple_norm, w_ple_gate, w_ple_proj):
    b, s, d = x.shape
    m = b * s
    assert d == D_MODEL and s % PROJ_TM == 0 and m % OUT_TM == 0 and s % RET_T == 0 and s % ATT_TQ == 0
    x2 = x.reshape(m, d)
    p2 = p[0].reshape(m, PLE_DIM)

    cos_t, sin_t = _rope_tables(s)
    dmat, xi_w, zeta_w, gdec = _decay_tables()
    qn_t = jnp.tile(diff_qn[0], LANES // DIFF_DH).reshape(1, LANES)
    kn_t = jnp.tile(diff_kn[0], LANES // DIFF_DH).reshape(1, LANES)

    params = functools.partial(pltpu.CompilerParams, vmem_limit_bytes=VMEM_LIMIT_BYTES)
    const = lambda *_: (0, 0)
    pos_blocks = s // PROJ_TM

    z = pl.pallas_call(
        _proj_kernel,
        grid=(m // PROJ_TM, D_IN // PROJ_TN),
        in_specs=[
            pl.BlockSpec((PROJ_TM, d), lambda i, j: (i, 0)),
            pl.BlockSpec((1, d), const),
            pl.BlockSpec((d, PROJ_TN), lambda i, j: (0, j)),
            pl.BlockSpec((PROJ_TM, LANES), lambda i, j: (i % pos_blocks, 0)),
            pl.BlockSpec((PROJ_TM, LANES), lambda i, j: (i % pos_blocks, 0)),
            pl.BlockSpec((1, LANES), const),
            pl.BlockSpec((1, LANES), const),
        ],
        out_specs=pl.BlockSpec((PROJ_TM, PROJ_TN), lambda i, j: (i, j)),
        out_shape=jax.ShapeDtypeStruct((m, D_IN), BF16),
        scratch_shapes=[pltpu.VMEM((PROJ_TM, d), BF16)],
        compiler_params=params(dimension_semantics=("arbitrary", "arbitrary")),
        name="proj",
    )(x2, attn_norm[0].reshape(1, d), w_in[0].astype(BF16), cos_t, sin_t, qn_t, kn_t)

    z3 = z.reshape(b, s, D_IN)
    n_pair = RET_HEADS // 2
    ro = pl.pallas_call(
        _retention_kernel,
        grid=(b, n_pair),
        in_specs=[
            pl.BlockSpec((None, s, LANES), lambda bi, hp: (bi, 0, RQ_BLK + hp)),
            pl.BlockSpec((None, s, LANES), lambda bi, hp: (bi, 0, RK_BLK + hp)),
            pl.BlockSpec((None, s, 2 * RET_DV), lambda bi, hp: (bi, 0, 1024 // 256 + hp)),
            pl.BlockSpec((None, s, 2 * RET_DV), lambda bi, hp: (bi, 0, 2048 // 256 + hp)),
            pl.BlockSpec((1, 2 * RET_DV), lambda bi, hp: (0, hp)),
            pl.BlockSpec((2, RET_T, RET_T), lambda bi, hp: (hp, 0, 0)),
            pl.BlockSpec((None, RET_T, 2 * RET_DV), lambda bi, hp: (hp, 0, 0)),
            pl.BlockSpec((None, RET_T, 2 * RET_DV), lambda bi, hp: (hp, 0, 0)),
            pl.BlockSpec((None, 2, 2 * RET_DK, 2 * RET_DV), lambda bi, hp: (hp, 0, 0, 0)),
        ],
        out_specs=pl.BlockSpec((None, s, 2 * RET_DV), lambda bi, hp: (bi, 0, hp)),
        out_shape=jax.ShapeDtypeStruct((b, s, RET_WIDTH), BF16),
        scratch_shapes=[pltpu.VMEM((2 * RET_DK, 2 * RET_DV), F32)],
        compiler_params=params(dimension_semantics=("arbitrary", "arbitrary")),
        name="retention",
    )(z3, z3, z3, z3, ret_gn[0].reshape(1, RET_WIDTH), dmat, xi_w, zeta_w, gdec)

    vec = lambda a: a[0].reshape(1, -1)
    tile_chunk = jnp.arange(ATT_TK) // CHUNK
    diag_bias = jnp.where(tile_chunk[:, None] <= tile_chunk[None, :], 0.0, NEG_BIG).astype(F32)
    diag_bias = jnp.tile(diag_bias, (1, 2))
    hw = ATT_HEADS * LANES
    assert (DQ_BLK % ATT_HEADS, DK_BLK % ATT_HEADS, DV_BLK % ATT_HEADS, DG_BLK % ATT_HEADS) == (0, 0, 0, 0)
    do = pl.pallas_call(
        _diffattn_kernel,
        grid=(b, DIFF_HEADS // ATT_HEADS),
        in_specs=[
            pl.BlockSpec((None, s, hw), lambda bi, h: (bi, 0, DQ_BLK // ATT_HEADS + h)),
            pl.BlockSpec((None, s, hw), lambda bi, h: (bi, 0, DK_BLK // ATT_HEADS + h)),
            pl.BlockSpec((None, s, hw), lambda bi, h: (bi, 0, DV_BLK // ATT_HEADS + h)),
            pl.BlockSpec((None, s, hw), lambda bi, h: (bi, 0, DG_BLK // ATT_HEADS + h)),
            pl.BlockSpec((ATT_TK, 2 * ATT_TQ), lambda *_: (0, 0)),
            pl.BlockSpec((1, DIFF_DH), lambda *_: (0, 0)),
            pl.BlockSpec((1, DIFF_DH), lambda *_: (0, 0)),
            pl.BlockSpec((1, DIFF_DH), lambda *_: (0, 0)),
            pl.BlockSpec((1, DIFF_DH), lambda *_: (0, 0)),
            pl.BlockSpec((1, DIFF_DV), lambda *_: (0, 0)),
        ],
        out_specs=pl.BlockSpec((None, s, hw), lambda bi, h: (bi, 0, h)),
        out_shape=jax.ShapeDtypeStruct((b, s, DIFF_WIDTH), BF16),
        scratch_shapes=[pltpu.VMEM((ATT_HEADS, DIFF_DV, s), BF16)],
        compiler_params=params(dimension_semantics=("arbitrary", "arbitrary")),
        name="diffattn",
    )(z3, z3, z3, z3, diag_bias, vec(diff_lq1), vec(diff_lk1), vec(diff_lq2), vec(diff_lk2), vec(diff_subln))

    resident = functools.partial(pl.BlockSpec, pipeline_mode=pl.Buffered(1))
    out = pl.pallas_call(
        _out_kernel,
        grid=(m // OUT_TM,),
        in_specs=[
            pl.BlockSpec((OUT_TM, d), lambda i: (i, 0)),
            pl.BlockSpec((OUT_TM, RET_WIDTH), lambda i: (i, 0)),
            pl.BlockSpec((OUT_TM, DIFF_WIDTH), lambda i: (i, 0)),
            pl.BlockSpec((OUT_TM, PLE_DIM), lambda i: (i, 0)),
            resident((d, d), lambda i: (0, 0)),
            pl.BlockSpec((1, d), lambda i: (0, 0)),
            resident((d, d), lambda i: (0, 0)),
            resident((PLE_DIM, d), lambda i: (0, 0)),
        ],
        out_specs=pl.BlockSpec((OUT_TM, d), lambda i: (i, 0)),
        out_shape=jax.ShapeDtypeStruct((m, d), F32),
        compiler_params=params(dimension_semantics=("arbitrary",)),
        name="outproj",
    )(x2, ro.reshape(m, RET_WIDTH), do.reshape(m, DIFF_WIDTH), p2, w_out[0].astype(BF16),
      ple_norm[0].reshape(1, d), w_ple_gate[0].astype(BF16), w_ple_proj[0].astype(BF16))

    return out.reshape(b, s, d)
```

```python
import functools
import math

import jax
import jax.numpy as jnp
from jax import lax
from jax.experimental import pallas as pl
from jax.experimental.pallas import tpu as pltpu

F32 = jnp.float32
BF16 = jnp.bfloat16

D_MODEL = 2048
CHUNK = 64
PLE_DIM = 256
RET_WIDTH = 1024
DIFF_WIDTH = 1024
RET_HEADS = 8
RET_DV = 128
RET_DK = 64
DIFF_HEADS = 8
DIFF_DV = 128
DIFF_DH = 64
ROPE_BASE = 10000.0
EPS = 1e-6
D_IN = 7168
LAM_INIT = 0.8 - 0.6 * math.exp(-0.3 * 0)

LANES = 128
VMEM_LIMIT_BYTES = 56 * 1024 * 1024

RQ_BLK = 0
RK_BLK = 512 // LANES
DQ_BLK = 3072 // LANES
DK_BLK = 4096 // LANES
DV_BLK = 5120 // LANES
DG_BLK = 6144 // LANES

PROJ_TM = 1024
PROJ_TN = 1024
ROW_CHUNK = 256
RET_T = 256
ATT_TQ = 256
ATT_TK = 256
ATT_HEADS = 2
OUT_TM = 512
OUT_CHUNK = 256
LOG2E = math.log2(math.e)
NEG_BIG = -1e30


def _silu(a):
    return a * (1.0 / (1.0 + jnp.exp(-a)))


def _swap_halves(a):
    lane = lax.broadcasted_iota(jnp.int32, a.shape, 1)
    first = (lane % RET_DK) < (RET_DK // 2)
    return jnp.where(first, pltpu.roll(a, LANES - RET_DK // 2, 1), pltpu.roll(a, RET_DK // 2, 1))


def _proj_kernel(x_ref, g_ref, w_ref, cos_ref, sin_ref, qn_ref, kn_ref, o_ref, u_ref, wb_ref):
    j = pl.program_id(1)
    tm = x_ref.shape[0]
    tn = w_ref.shape[1]

    def rope_epilogue(a, rows):
        c = cos_ref[rows, :]
        s = sin_ref[rows, :]
        for sl in range(tn // LANES):
            scale = 1.0 if sl * LANES < RET_HEADS * RET_DK else RET_DK ** -0.5
            asl = a[:, sl * LANES:(sl + 1) * LANES]
            o_ref[rows, pl.ds(sl * LANES, LANES)] = ((asl * c + _swap_halves(asl) * s) * scale).astype(BF16)

    def norm_epilogue(gn_ref, scale):
        def apply(a, rows):
            low = lax.broadcasted_iota(jnp.int32, (1, LANES), 1) < DIFF_DH
            for sl in range(tn // LANES):
                asl = a[:, sl * LANES:(sl + 1) * LANES]
                sq = asl * asl
                s_lo = jnp.sum(jnp.where(low, sq, 0.0), axis=-1, keepdims=True)
                s_hi = jnp.sum(jnp.where(low, 0.0, sq), axis=-1, keepdims=True)
                ms = jnp.where(low, s_lo, s_hi) * (1.0 / DIFF_DH)
                o_ref[rows, pl.ds(sl * LANES, LANES)] = (asl * lax.rsqrt(ms + EPS) * gn_ref[...] * scale).astype(BF16)
        return apply

    def silu_epilogue(a, rows):
        o_ref[rows, :] = _silu(a).astype(BF16)

    def plain_epilogue(a, rows):
        o_ref[rows, :] = a.astype(BF16)

    def run(epilogue, normalize_input=False):
        sizes = [ROW_CHUNK] * (tm // ROW_CHUNK - 1) + [ROW_CHUNK // 2] * 2
        for kt in range(w_ref.shape[0] // ROW_CHUNK):
            krows = pl.ds(kt * ROW_CHUNK, ROW_CHUNK)
            wb_ref[krows, :] = w_ref[krows, :].astype(BF16)
        for r, size in enumerate(sizes):
            rows = pl.ds(sum(sizes[:r]), size)
            if normalize_input:
                x = x_ref[rows, :]
                ms = jnp.mean(x * x, axis=-1, keepdims=True)
                u_ref[rows, :] = (x * lax.rsqrt(ms + EPS) * g_ref[...]).astype(BF16)
            epilogue(jnp.dot(u_ref[rows, :], wb_ref[...], preferred_element_type=F32), rows)

    @pl.when(j == 0)
    def _():
        run(rope_epilogue, normalize_input=True)

    @pl.when((j == 1) | (j == 5))
    def _():
        run(plain_epilogue)

    @pl.when((j == 2) | (j == 6))
    def _():
        run(silu_epilogue)

    @pl.when(j == 3)
    def _():
        run(norm_epilogue(qn_ref, DIFF_DH ** -0.5 * LOG2E))

    @pl.when(j == 4)
    def _():
        run(norm_epilogue(kn_ref, 1.0))


def _retention_kernel(q_ref, k_ref, v_ref, g_ref, gn_ref, dmat_ref, xi_ref, zeta_ref, gdec_ref, o_ref, state_ref):
    s_len = q_ref.shape[0]
    lane = lax.broadcasted_iota(jnp.int32, (1, LANES), 1)
    state_ref[...] = jnp.zeros_like(state_ref)
    for n in range(s_len // RET_T):
        rows = pl.ds(n * RET_T, RET_T)
        q = q_ref[rows, :]
        k = k_ref[rows, :]
        v = v_ref[rows, :]
        cross = jnp.dot(q, state_ref[...].astype(BF16), preferred_element_type=F32)
        vz = (v.astype(F32) * zeta_ref[...]).astype(BF16)
        kv = lax.dot_general(k, vz, (((0,), (0,)), ((), ())), preferred_element_type=F32)
        state_ref[...] = state_ref[...] * gdec_ref[0] + kv * gdec_ref[1]
        for hh in range(2):
            cols = pl.ds(hh * RET_DV, RET_DV)
            qm = jnp.where((lane < RET_DK) == (hh == 0), q, jnp.zeros_like(q))
            sc = lax.dot_general(qm, k, (((1,), (1,)), ((), ())), preferred_element_type=F32)
            sc = sc * dmat_ref[hh]
            intra = jnp.dot(sc.astype(BF16), v[:, hh * RET_DV:(hh + 1) * RET_DV], preferred_element_type=F32)
            o = intra + cross[:, hh * RET_DV:(hh + 1) * RET_DV] * xi_ref[:, cols]
            ms = jnp.mean(o * o, axis=-1, keepdims=True)
            o = o * lax.rsqrt(ms + EPS) * gn_ref[:, cols]
            o_ref[rows, cols] = (o * g_ref[rows, cols].astype(F32)).astype(BF16)


def _diffattn_kernel(q_ref, k_ref, v_ref, g_ref, bias_ref, lq1_ref, lk1_ref, lq2_ref, lk2_ref, sub_ref, o_ref,
                     vt_ref):
    s_len = k_ref.shape[0]
    tq = ATT_TQ
    for hh in range(ATT_HEADS):
        for t in range(s_len // ATT_TK):
            cols = pl.ds(t * ATT_TK, ATT_TK)
            vt_ref[hh, :, cols] = v_ref[cols, hh * DIFF_DV:(hh + 1) * DIFF_DV].astype(F32).T.astype(BF16)

    lam = (jnp.exp(jnp.sum(lq1_ref[...] * lk1_ref[...], axis=-1, keepdims=True))
           - jnp.exp(jnp.sum(lq2_ref[...] * lk2_ref[...], axis=-1, keepdims=True)) + LAM_INIT)

    def scores(hh, qi):
        lo = qi * tq
        hcols = slice(hh * LANES, (hh + 1) * LANES)
        qt = q_ref[lo:lo + tq, hcols].astype(F32).T
        row = lax.broadcasted_iota(jnp.int32, qt.shape, 0)
        qbd = jnp.concatenate([jnp.where(row < DIFF_DH, qt, 0.0), jnp.where(row >= DIFF_DH, qt, 0.0)],
                              axis=1).astype(BF16)
        sd = jnp.dot(k_ref[lo:lo + tq, hcols], qbd, preferred_element_type=F32) + bias_ref[...]
        m = jnp.max(sd, axis=0, keepdims=True)
        so = None
        if qi > 0:
            so = jnp.dot(k_ref[0:lo, hcols], qbd, preferred_element_type=F32)
            m = jnp.maximum(m, jnp.max(so, axis=0, keepdims=True))
        return sd, so, m

    def finish(hh, qi, sd, so, m):
        lo = qi * tq
        hcols = slice(hh * LANES, (hh + 1) * LANES)
        pd = jnp.exp2(sd - m)
        l = jnp.sum(pd, axis=0, keepdims=True)
        p = pd.astype(BF16)
        if qi > 0:
            po = jnp.exp2(so - m)
            l = l + jnp.sum(po, axis=0, keepdims=True)
            p = jnp.concatenate([po.astype(BF16), p], axis=0)
        on = jnp.dot(vt_ref[hh, :, 0:lo + tq], p, preferred_element_type=F32) / l
        o = (on[:, 0:tq] - lam * on[:, tq:2 * tq]).T
        ms_o = jnp.mean(o * o, axis=-1, keepdims=True)
        o = o * lax.rsqrt(ms_o + EPS) * sub_ref[...] * (1.0 - LAM_INIT)
        o_ref[lo:lo + tq, hcols] = (o * g_ref[lo:lo + tq, hcols].astype(F32)).astype(BF16)

    order = [(hh, qi) for hh in range(ATT_HEADS) for qi in reversed(range(s_len // tq))]
    cur = scores(*order[0])
    for n, unit in enumerate(order):
        nxt = scores(*order[n + 1]) if n + 1 < len(order) else None
        finish(*unit, *cur)
        cur = nxt


def _out_kernel(x_ref, ro_ref, do_ref, p_ref, wo_ref, pn_ref, wg_ref, wp_ref, o_ref):
    tm = x_ref.shape[0]

    def residual(rows):
        acc = jnp.dot(ro_ref[rows, :], wo_ref[0:RET_WIDTH, :], preferred_element_type=F32)
        acc = acc + jnp.dot(do_ref[rows, :], wo_ref[RET_WIDTH:, :], preferred_element_type=F32)
        h1 = x_ref[rows, :] + acc
        ms = jnp.mean(h1 * h1, axis=-1, keepdims=True)
        return h1, (h1 * lax.rsqrt(ms + EPS) * pn_ref[...]).astype(BF16)

    def gated(rows, h1, hn):
        z = jnp.dot(hn, wg_ref[...], preferred_element_type=F32)
        gate = 1.0 / (1.0 + jnp.exp(-z))
        ple = jnp.dot(p_ref[rows, :].astype(BF16), wp_ref[...], preferred_element_type=F32)
        o_ref[rows, :] = h1 + gate * ple

    chunks = [pl.ds(r * OUT_CHUNK, OUT_CHUNK) for r in range(tm // OUT_CHUNK)]
    cur = residual(chunks[0])
    for n, rows in enumerate(chunks):
        nxt = residual(chunks[n + 1]) if n + 1 < len(chunks) else None
        gated(rows, *cur)
        cur = nxt


def _decay_tables():
    log_g = jnp.log1p(-jnp.exp2(-5.0 - jnp.arange(RET_HEADS, dtype=F32)))
    idx = jnp.arange(RET_T, dtype=F32)
    dist = jnp.abs(idx[:, None] - idx[None, :])
    same_or_earlier = (jnp.arange(RET_T)[None, :] // CHUNK) <= (jnp.arange(RET_T)[:, None] // CHUNK)
    dmat = jnp.where(same_or_earlier[None], jnp.exp(dist[None] * log_g[:, None, None]), 0.0)
    xi = jnp.exp((idx + 1.0)[None, :] * log_g[:, None])
    zeta = jnp.exp((RET_T - 1.0 - idx)[None, :] * log_g[:, None])
    g_step = jnp.exp(RET_T * log_g)
    n_pair = RET_HEADS // 2
    widen = lambda t: jnp.broadcast_to(t.reshape(n_pair, 2, RET_T, 1), (n_pair, 2, RET_T, RET_DV)) \
        .transpose(0, 2, 1, 3).reshape(n_pair, RET_T, 2 * RET_DV)
    xi_w = widen(xi)
    zeta_w = widen(zeta)
    rowh = (jnp.arange(2 * RET_DK) // RET_DK)[:, None]
    colh = (jnp.arange(2 * RET_DV) // RET_DV)[None, :]
    diag = (rowh == colh).astype(F32)
    gp = g_step.reshape(n_pair, 2)
    gq = jnp.where(colh[None] == 0, gp[:, 0, None, None], gp[:, 1, None, None]) * diag[None]
    gdec = jnp.stack([gq, jnp.broadcast_to(diag[None], gq.shape)], axis=1)
    return dmat, xi_w, zeta_w, gdec


def _rope_tables(s_len):
    pos = jnp.arange(s_len, dtype=F32)
    inv_freq = ROPE_BASE ** (-jnp.arange(RET_DK // 2, dtype=F32) / (RET_DK // 2))
    ang = pos[:, None] * inv_freq[None, :]
    cos, sin = jnp.cos(ang), jnp.sin(ang)
    cos_t = jnp.tile(cos, (1, LANES // (RET_DK // 2)))
    sin_t = jnp.tile(jnp.concatenate([-sin, sin], axis=-1), (1, LANES // RET_DK))
    return cos_t, sin_t


def kernel(x, p, attn_norm, w_in, ret_gn, diff_qn, diff_kn, diff_lq1, diff_lk1, diff_lq2, diff_lk2, diff_subln, w_out, ple_norm, w_ple_gate, w_ple_proj):
    b, s, d = x.shape
    m = b * s
    assert d == D_MODEL and s % PROJ_TM == 0 and m % OUT_TM == 0 and s % RET_T == 0 and s % ATT_TQ == 0
    x2 = x.reshape(m, d)
    p2 = p[0].reshape(m, PLE_DIM)

    cos_t, sin_t = _rope_tables(s)
    dmat, xi_w, zeta_w, gdec = _decay_tables()
    qn_t = jnp.tile(diff_qn[0], LANES // DIFF_DH).reshape(1, LANES)
    kn_t = jnp.tile(diff_kn[0], LANES // DIFF_DH).reshape(1, LANES)

    params = functools.partial(pltpu.CompilerParams, vmem_limit_bytes=VMEM_LIMIT_BYTES)
    const = lambda *_: (0, 0)
    pos_blocks = s // PROJ_TM

    z = pl.pallas_call(
        _proj_kernel,
        grid=(m // PROJ_TM, D_IN // PROJ_TN),
        in_specs=[
            pl.BlockSpec((PROJ_TM, d), lambda i, j: (i, 0)),
            pl.BlockSpec((1, d), const),
            pl.BlockSpec((d, PROJ_TN), lambda i, j: (0, j)),
            pl.BlockSpec((PROJ_TM, LANES), lambda i, j: (i % pos_blocks, 0)),
            pl.BlockSpec((PROJ_TM, LANES), lambda i, j: (i % pos_blocks, 0)),
            pl.BlockSpec((1, LANES), const),
            pl.BlockSpec((1, LANES), const),
        ],
        out_specs=pl.BlockSpec((PROJ_TM, PROJ_TN), lambda i, j: (i, j)),
        out_shape=jax.ShapeDtypeStruct((m, D_IN), BF16),
        scratch_shapes=[pltpu.VMEM((PROJ_TM, d), BF16), pltpu.VMEM((d, PROJ_TN), BF16)],
        compiler_params=params(dimension_semantics=("arbitrary", "arbitrary")),
        name="proj",
    )(x2, attn_norm[0].reshape(1, d), w_in[0], cos_t, sin_t, qn_t, kn_t)

    z3 = z.reshape(b, s, D_IN)
    n_pair = RET_HEADS // 2
    ro = pl.pallas_call(
        _retention_kernel,
        grid=(b, n_pair),
        in_specs=[
            pl.BlockSpec((None, s, LANES), lambda bi, hp: (bi, 0, RQ_BLK + hp)),
            pl.BlockSpec((None, s, LANES), lambda bi, hp: (bi, 0, RK_BLK + hp)),
            pl.BlockSpec((None, s, 2 * RET_DV), lambda bi, hp: (bi, 0, 1024 // 256 + hp)),
            pl.BlockSpec((None, s, 2 * RET_DV), lambda bi, hp: (bi, 0, 2048 // 256 + hp)),
            pl.BlockSpec((1, 2 * RET_DV), lambda bi, hp: (0, hp)),
            pl.BlockSpec((2, RET_T, RET_T), lambda bi, hp: (hp, 0, 0)),
            pl.BlockSpec((None, RET_T, 2 * RET_DV), lambda bi, hp: (hp, 0, 0)),
            pl.BlockSpec((None, RET_T, 2 * RET_DV), lambda bi, hp: (hp, 0, 0)),
            pl.BlockSpec((None, 2, 2 * RET_DK, 2 * RET_DV), lambda bi, hp: (hp, 0, 0, 0)),
        ],
        out_specs=pl.BlockSpec((None, s, 2 * RET_DV), lambda bi, hp: (bi, 0, hp)),
        out_shape=jax.ShapeDtypeStruct((b, s, RET_WIDTH), BF16),
        scratch_shapes=[pltpu.VMEM((2 * RET_DK, 2 * RET_DV), F32)],
        compiler_params=params(dimension_semantics=("arbitrary", "arbitrary")),
        name="retention",
    )(z3, z3, z3, z3, ret_gn[0].reshape(1, RET_WIDTH), dmat, xi_w, zeta_w, gdec)

    vec = lambda a: a[0].reshape(1, -1)
    tile_chunk = jnp.arange(ATT_TK) // CHUNK
    diag_bias = jnp.where(tile_chunk[:, None] <= tile_chunk[None, :], 0.0, NEG_BIG).astype(F32)
    diag_bias = jnp.tile(diag_bias, (1, 2))
    hw = ATT_HEADS * LANES
    assert (DQ_BLK % ATT_HEADS, DK_BLK % ATT_HEADS, DV_BLK % ATT_HEADS, DG_BLK % ATT_HEADS) == (0, 0, 0, 0)
    do = pl.pallas_call(
        _diffattn_kernel,
        grid=(b, DIFF_HEADS // ATT_HEADS),
        in_specs=[
            pl.BlockSpec((None, s, hw), lambda bi, h: (bi, 0, DQ_BLK // ATT_HEADS + h)),
            pl.BlockSpec((None, s, hw), lambda bi, h: (bi, 0, DK_BLK // ATT_HEADS + h)),
            pl.BlockSpec((None, s, hw), lambda bi, h: (bi, 0, DV_BLK // ATT_HEADS + h)),
            pl.BlockSpec((None, s, hw), lambda bi, h: (bi, 0, DG_BLK // ATT_HEADS + h)),
            pl.BlockSpec((ATT_TK, 2 * ATT_TQ), lambda *_: (0, 0)),
            pl.BlockSpec((1, DIFF_DH), lambda *_: (0, 0)),
            pl.BlockSpec((1, DIFF_DH), lambda *_: (0, 0)),
            pl.BlockSpec((1, DIFF_DH), lambda *_: (0, 0)),
            pl.BlockSpec((1, DIFF_DH), lambda *_: (0, 0)),
            pl.BlockSpec((1, DIFF_DV), lambda *_: (0, 0)),
        ],
        out_specs=pl.BlockSpec((None, s, hw), lambda bi, h: (bi, 0, h)),
        out_shape=jax.ShapeDtypeStruct((b, s, DIFF_WIDTH), BF16),
        scratch_shapes=[pltpu.VMEM((ATT_HEADS, DIFF_DV, s), BF16)],
        compiler_params=params(dimension_semantics=("arbitrary", "arbitrary")),
        name="diffattn",
    )(z3, z3, z3, z3, diag_bias, vec(diff_lq1), vec(diff_lk1), vec(diff_lq2), vec(diff_lk2), vec(diff_subln))

    resident = functools.partial(pl.BlockSpec, pipeline_mode=pl.Buffered(1))
    out = pl.pallas_call(
        _out_kernel,
        grid=(m // OUT_TM,),
        in_specs=[
            pl.BlockSpec((OUT_TM, d), lambda i: (i, 0)),
            pl.BlockSpec((OUT_TM, RET_WIDTH), lambda i: (i, 0)),
            pl.BlockSpec((OUT_TM, DIFF_WIDTH), lambda i: (i, 0)),
            pl.BlockSpec((OUT_TM, PLE_DIM), lambda i: (i, 0)),
            resident((d, d), lambda i: (0, 0)),
            pl.BlockSpec((1, d), lambda i: (0, 0)),
            resident((d, d), lambda i: (0, 0)),
            resident((PLE_DIM, d), lambda i: (0, 0)),
        ],
        out_specs=pl.BlockSpec((OUT_TM, d), lambda i: (i, 0)),
        out_shape=jax.ShapeDtypeStruct((m, d), F32),
        compiler_params=params(dimension_semantics=("arbitrary",)),
        name="outproj",
    )(x2, ro.reshape(m, RET_WIDTH), do.reshape(m, DIFF_WIDTH), p2, w_out[0].astype(BF16),
      ple_norm[0].reshape(1, d), w_ple_gate[0].astype(BF16), w_ple_proj[0].astype(BF16))

    return out.reshape(b, s, d)
```

```python
import functools
import math

import jax
import jax.numpy as jnp
import numpy as np
from jax import lax
from jax.experimental import pallas as pl
from jax.experimental.pallas import tpu as pltpu

F32 = jnp.float32
BF16 = jnp.bfloat16

D_MODEL = 2048
CHUNK = 64
PLE_DIM = 256
RET_WIDTH = 1024
DIFF_WIDTH = 1024
RET_HEADS = 8
RET_DV = 128
RET_DK = 64
DIFF_HEADS = 8
DIFF_DV = 128
DIFF_DH = 64
ROPE_BASE = 10000.0
EPS = 1e-6
D_IN = 7168
LAM_INIT = 0.8 - 0.6 * math.exp(-0.3 * 0)

LANES = 128
VMEM_LIMIT_BYTES = 56 * 1024 * 1024

RQ_BLK = 0
RK_BLK = 512 // LANES
DQ_BLK = 3072 // LANES
DK_BLK = 4096 // LANES
DV_BLK = 5120 // LANES
DG_BLK = 6144 // LANES

PROJ_TM = 1024
PROJ_TN = 1024
ROW_CHUNK = 256
RET_T = 256
ATT_TQ = 256
ATT_TK = 256
ATT_HEADS = 2
OUT_TM = 512
OUT_CHUNK = 256
LOG2E = math.log2(math.e)
NEG_BIG = -1e30


def _silu(a):
    return a * (1.0 / (1.0 + jnp.exp(-a)))


def _swap_halves(a):
    lane = lax.broadcasted_iota(jnp.int32, a.shape, 1)
    first = (lane % RET_DK) < (RET_DK // 2)
    return jnp.where(first, pltpu.roll(a, LANES - RET_DK // 2, 1), pltpu.roll(a, RET_DK // 2, 1))


def _proj_kernel(x_ref, g_ref, w_ref, cos_ref, sin_ref, qn_ref, kn_ref, o_ref, u_ref, wb_ref):
    j = pl.program_id(1)
    tm = x_ref.shape[0]
    tn = w_ref.shape[1]

    def rope_epilogue(a, rows):
        c = cos_ref[rows, :]
        s = sin_ref[rows, :]
        for sl in range(tn // LANES):
            scale = 1.0 if sl * LANES < RET_HEADS * RET_DK else RET_DK ** -0.5
            asl = a[:, sl * LANES:(sl + 1) * LANES]
            o_ref[rows, pl.ds(sl * LANES, LANES)] = ((asl * c + _swap_halves(asl) * s) * scale).astype(BF16)

    def norm_epilogue(gn_ref, scale):
        def apply(a, rows):
            low = lax.broadcasted_iota(jnp.int32, (1, LANES), 1) < DIFF_DH
            for sl in range(tn // LANES):
                asl = a[:, sl * LANES:(sl + 1) * LANES]
                sq = asl * asl
                s_lo = jnp.sum(jnp.where(low, sq, 0.0), axis=-1, keepdims=True)
                s_hi = jnp.sum(jnp.where(low, 0.0, sq), axis=-1, keepdims=True)
                ms = jnp.where(low, s_lo, s_hi) * (1.0 / DIFF_DH)
                o_ref[rows, pl.ds(sl * LANES, LANES)] = (asl * lax.rsqrt(ms + EPS) * gn_ref[...] * scale).astype(BF16)
        return apply

    def silu_epilogue(a, rows):
        o_ref[rows, :] = _silu(a).astype(BF16)

    def plain_epilogue(a, rows):
        o_ref[rows, :] = a.astype(BF16)

    def run(epilogue, normalize_input=False):
        sizes = [ROW_CHUNK] * (tm // ROW_CHUNK - 1) + [ROW_CHUNK // 2] * 2
        for kt in range(w_ref.shape[0] // ROW_CHUNK):
            krows = pl.ds(kt * ROW_CHUNK, ROW_CHUNK)
            wb_ref[krows, :] = w_ref[krows, :].astype(BF16)
        for r, size in enumerate(sizes):
            rows = pl.ds(sum(sizes[:r]), size)
            if normalize_input:
                x = x_ref[rows, :]
                ms = jnp.mean(x * x, axis=-1, keepdims=True)
                u_ref[rows, :] = (x * lax.rsqrt(ms + EPS) * g_ref[...]).astype(BF16)
            epilogue(jnp.dot(u_ref[rows, :], wb_ref[...], preferred_element_type=F32), rows)

    @pl.when(j == 0)
    def _():
        run(rope_epilogue, normalize_input=True)

    @pl.when((j == 1) | (j == 5))
    def _():
        run(plain_epilogue)

    @pl.when((j == 2) | (j == 6))
    def _():
        run(silu_epilogue)

    @pl.when(j == 3)
    def _():
        run(norm_epilogue(qn_ref, DIFF_DH ** -0.5 * LOG2E))

    @pl.when(j == 4)
    def _():
        run(norm_epilogue(kn_ref, 1.0))


def _retention_kernel(q_ref, k_ref, v_ref, g_ref, gn_ref, dmat_ref, xi_ref, zeta_ref, gdec_ref, o_ref):
    s_len = q_ref.shape[0]
    lane = lax.broadcasted_iota(jnp.int32, (1, LANES), 1)

    def front(n):
        rows = pl.ds(n * RET_T, RET_T)
        q = q_ref[rows, :]
        k = k_ref[rows, :]
        vz = (v_ref[rows, :].astype(F32) * zeta_ref[...]).astype(BF16)
        kv = lax.dot_general(k, vz, (((0,), (0,)), ((), ())), preferred_element_type=F32)
        scs = []
        for hh in range(2):
            qm = jnp.where((lane < RET_DK) == (hh == 0), q, jnp.zeros_like(q))
            sc = lax.dot_general(qm, k, (((1,), (1,)), ((), ())), preferred_element_type=F32)
            scs.append((sc * dmat_ref[hh]).astype(BF16))
        return kv, scs

    def back(n, scs, state):
        rows = pl.ds(n * RET_T, RET_T)
        cross = jnp.dot(q_ref[rows, :], state.astype(BF16), preferred_element_type=F32)
        for hh in range(2):
            cols = pl.ds(hh * RET_DV, RET_DV)
            intra = jnp.dot(scs[hh], v_ref[rows, cols], preferred_element_type=F32)
            o = intra + cross[:, hh * RET_DV:(hh + 1) * RET_DV] * xi_ref[:, cols]
            ms = jnp.mean(o * o, axis=-1, keepdims=True)
            o = o * lax.rsqrt(ms + EPS) * gn_ref[:, cols]
            o_ref[rows, cols] = (o * g_ref[rows, cols].astype(F32)).astype(BF16)

    n_steps = s_len // RET_T
    state = jnp.zeros((2 * RET_DK, 2 * RET_DV), F32)
    cur = front(0)
    for n in range(n_steps):
        nxt = front(n + 1) if n + 1 < n_steps else None
        kv, scs = cur
        back(n, scs, state)
        state = state * gdec_ref[0] + kv * gdec_ref[1]
        cur = nxt


def _diffattn_kernel(q_ref, k_ref, v_ref, g_ref, bias_ref, lq1_ref, lk1_ref, lq2_ref, lk2_ref, sub_ref, o_ref,
                     vt_ref):
    s_len = k_ref.shape[0]
    tq = ATT_TQ
    for hh in range(ATT_HEADS):
        for t in range(s_len // ATT_TK):
            cols = pl.ds(t * ATT_TK, ATT_TK)
            vt_ref[hh, :, cols] = v_ref[cols, hh * DIFF_DV:(hh + 1) * DIFF_DV].astype(F32).T.astype(BF16)

    lam = (jnp.exp(jnp.sum(lq1_ref[...] * lk1_ref[...], axis=-1, keepdims=True))
           - jnp.exp(jnp.sum(lq2_ref[...] * lk2_ref[...], axis=-1, keepdims=True)) + LAM_INIT)

    def scores(hh, qi):
        lo = qi * tq
        hcols = slice(hh * LANES, (hh + 1) * LANES)
        qt = q_ref[lo:lo + tq, hcols].astype(F32).T
        row = lax.broadcasted_iota(jnp.int32, qt.shape, 0)
        qbd = jnp.concatenate([jnp.where(row < DIFF_DH, qt, 0.0), jnp.where(row >= DIFF_DH, qt, 0.0)],
                              axis=1).astype(BF16)
        sd = jnp.dot(k_ref[lo:lo + tq, hcols], qbd, preferred_element_type=F32) + bias_ref[...]
        m = jnp.max(sd, axis=0, keepdims=True)
        so = None
        if qi > 0:
            so = jnp.dot(k_ref[0:lo, hcols], qbd, preferred_element_type=F32)
            m = jnp.maximum(m, jnp.max(so, axis=0, keepdims=True))
        return sd, so, m

    def finish(hh, qi, sd, so, m):
        lo = qi * tq
        hcols = slice(hh * LANES, (hh + 1) * LANES)
        pd = jnp.exp2(sd - m)
        l = jnp.sum(pd, axis=0, keepdims=True)
        p = pd.astype(BF16)
        if qi > 0:
            po = jnp.exp2(so - m)
            l = l + jnp.sum(po, axis=0, keepdims=True)
            p = jnp.concatenate([po.astype(BF16), p], axis=0)
        on = jnp.dot(vt_ref[hh, :, 0:lo + tq], p, preferred_element_type=F32) / l
        o = (on[:, 0:tq] - lam * on[:, tq:2 * tq]).T
        ms_o = jnp.mean(o * o, axis=-1, keepdims=True)
        o = o * lax.rsqrt(ms_o + EPS) * sub_ref[...] * (1.0 - LAM_INIT)
        o_ref[lo:lo + tq, hcols] = (o * g_ref[lo:lo + tq, hcols].astype(F32)).astype(BF16)

    order = [(hh, qi) for hh in range(ATT_HEADS) for qi in reversed(range(s_len // tq))]
    cur = scores(*order[0])
    for n, unit in enumerate(order):
        nxt = scores(*order[n + 1]) if n + 1 < len(order) else None
        finish(*unit, *cur)
        cur = nxt


def _out_kernel(x_ref, ro_ref, do_ref, p_ref, wo_ref, pn_ref, wg_ref, wp_ref, o_ref):
    tm = x_ref.shape[0]

    def residual(rows):
        acc = jnp.dot(ro_ref[rows, :], wo_ref[0:RET_WIDTH, :], preferred_element_type=F32)
        acc = acc + jnp.dot(do_ref[rows, :], wo_ref[RET_WIDTH:, :], preferred_element_type=F32)
        h1 = x_ref[rows, :] + acc
        ms = jnp.mean(h1 * h1, axis=-1, keepdims=True)
        return h1, (h1 * lax.rsqrt(ms + EPS) * pn_ref[...]).astype(BF16)

    def gated(rows, h1, hn):
        z = jnp.dot(hn, wg_ref[...], preferred_element_type=F32)
        gate = 1.0 / (1.0 + jnp.exp(-z))
        ple = jnp.dot(p_ref[rows, :].astype(BF16), wp_ref[...], preferred_element_type=F32)
        o_ref[rows, :] = h1 + gate * ple

    chunks = [pl.ds(r * OUT_CHUNK, OUT_CHUNK) for r in range(tm // OUT_CHUNK)]
    cur = residual(chunks[0])
    for n, rows in enumerate(chunks):
        nxt = residual(chunks[n + 1]) if n + 1 < len(chunks) else None
        gated(rows, *cur)
        cur = nxt


def _decay_tables():
    log_g = np.log1p(-np.exp2(-5.0 - np.arange(RET_HEADS, dtype=np.float64)))
    idx = np.arange(RET_T, dtype=np.float64)
    dist = np.abs(idx[:, None] - idx[None, :])
    same_or_earlier = (np.arange(RET_T)[None, :] // CHUNK) <= (np.arange(RET_T)[:, None] // CHUNK)
    dmat = np.where(same_or_earlier[None], np.exp(dist[None] * log_g[:, None, None]), 0.0)
    xi = np.exp((idx + 1.0)[None, :] * log_g[:, None])
    zeta = np.exp((RET_T - 1.0 - idx)[None, :] * log_g[:, None])
    g_step = np.exp(RET_T * log_g)
    n_pair = RET_HEADS // 2
    widen = lambda t: np.broadcast_to(t.reshape(n_pair, 2, RET_T, 1), (n_pair, 2, RET_T, RET_DV)) \
        .transpose(0, 2, 1, 3).reshape(n_pair, RET_T, 2 * RET_DV)
    xi_w = widen(xi)
    zeta_w = widen(zeta)
    rowh = (np.arange(2 * RET_DK) // RET_DK)[:, None]
    colh = (np.arange(2 * RET_DV) // RET_DV)[None, :]
    diag = (rowh == colh).astype(np.float64)
    gp = g_step.reshape(n_pair, 2)
    gq = np.where(colh[None] == 0, gp[:, 0, None, None], gp[:, 1, None, None]) * diag[None]
    gdec = np.stack([gq, np.broadcast_to(diag[None], gq.shape)], axis=1)
    return tuple(jnp.asarray(t.astype(np.float32)) for t in (dmat, xi_w, zeta_w, gdec))


def _rope_tables(s_len):
    pos = np.arange(s_len, dtype=np.float64)
    inv_freq = ROPE_BASE ** (-np.arange(RET_DK // 2, dtype=np.float64) / (RET_DK // 2))
    ang = pos[:, None] * inv_freq[None, :]
    cos, sin = np.cos(ang), np.sin(ang)
    cos_t = np.tile(cos, (1, LANES // (RET_DK // 2)))
    sin_t = np.tile(np.concatenate([-sin, sin], axis=-1), (1, LANES // RET_DK))
    return jnp.asarray(cos_t.astype(np.float32)), jnp.asarray(sin_t.astype(np.float32))


def kernel(x, p, attn_norm, w_in, ret_gn, diff_qn, diff_kn, diff_lq1, diff_lk1, diff_lq2, diff_lk2, diff_subln, w_out, ple_norm, w_ple_gate, w_ple_proj):
    b, s, d = x.shape
    m = b * s
    assert d == D_MODEL and s % PROJ_TM == 0 and m % OUT_TM == 0 and s % RET_T == 0 and s % ATT_TQ == 0
    x2 = x.reshape(m, d)
    p2 = p[0].reshape(m, PLE_DIM)

    cos_t, sin_t = _rope_tables(s)
    dmat, xi_w, zeta_w, gdec = _decay_tables()
    qn_t = jnp.tile(diff_qn[0], LANES // DIFF_DH).reshape(1, LANES)
    kn_t = jnp.tile(diff_kn[0], LANES // DIFF_DH).reshape(1, LANES)

    params = functools.partial(pltpu.CompilerParams, vmem_limit_bytes=VMEM_LIMIT_BYTES)
    const = lambda *_: (0, 0)
    pos_blocks = s // PROJ_TM

    z = pl.pallas_call(
        _proj_kernel,
        grid=(m // PROJ_TM, D_IN // PROJ_TN),
        in_specs=[
            pl.BlockSpec((PROJ_TM, d), lambda i, j: (i, 0)),
            pl.BlockSpec((1, d), const),
            pl.BlockSpec((d, PROJ_TN), lambda i, j: (0, j)),
            pl.BlockSpec((PROJ_TM, LANES), lambda i, j: (i % pos_blocks, 0)),
            pl.BlockSpec((PROJ_TM, LANES), lambda i, j: (i % pos_blocks, 0)),
            pl.BlockSpec((1, LANES), const),
            pl.BlockSpec((1, LANES), const),
        ],
        out_specs=pl.BlockSpec((PROJ_TM, PROJ_TN), lambda i, j: (i, j)),
        out_shape=jax.ShapeDtypeStruct((m, D_IN), BF16),
        scratch_shapes=[pltpu.VMEM((PROJ_TM, d), BF16), pltpu.VMEM((d, PROJ_TN), BF16)],
        compiler_params=params(dimension_semantics=("arbitrary", "arbitrary")),
        name="proj",
    )(x2, attn_norm[0].reshape(1, d), w_in[0], cos_t, sin_t, qn_t, kn_t)

    z3 = z.reshape(b, s, D_IN)
    n_pair = RET_HEADS // 2
    ro = pl.pallas_call(
        _retention_kernel,
        grid=(b, n_pair),
        in_specs=[
            pl.BlockSpec((None, s, LANES), lambda bi, hp: (bi, 0, RQ_BLK + hp)),
            pl.BlockSpec((None, s, LANES), lambda bi, hp: (bi, 0, RK_BLK + hp)),
            pl.BlockSpec((None, s, 2 * RET_DV), lambda bi, hp: (bi, 0, 1024 // 256 + hp)),
            pl.BlockSpec((None, s, 2 * RET_DV), lambda bi, hp: (bi, 0, 2048 // 256 + hp)),
            pl.BlockSpec((1, 2 * RET_DV), lambda bi, hp: (0, hp)),
            pl.BlockSpec((2, RET_T, RET_T), lambda bi, hp: (hp, 0, 0)),
            pl.BlockSpec((None, RET_T, 2 * RET_DV), lambda bi, hp: (hp, 0, 0)),
            pl.BlockSpec((None, RET_T, 2 * RET_DV), lambda bi, hp: (hp, 0, 0)),
            pl.BlockSpec((None, 2, 2 * RET_DK, 2 * RET_DV), lambda bi, hp: (hp, 0, 0, 0)),
        ],
        out_specs=pl.BlockSpec((None, s, 2 * RET_DV), lambda bi, hp: (bi, 0, hp)),
        out_shape=jax.ShapeDtypeStruct((b, s, RET_WIDTH), BF16),
        compiler_params=params(dimension_semantics=("arbitrary", "arbitrary")),
        name="retention",
    )(z3, z3, z3, z3, ret_gn[0].reshape(1, RET_WIDTH), dmat, xi_w, zeta_w, gdec)

    vec = lambda a: a[0].reshape(1, -1)
    tile_chunk = np.arange(ATT_TK) // CHUNK
    diag_bias = np.where(tile_chunk[:, None] <= tile_chunk[None, :], 0.0, NEG_BIG).astype(np.float32)
    diag_bias = jnp.asarray(np.tile(diag_bias, (1, 2)))
    hw = ATT_HEADS * LANES
    assert (DQ_BLK % ATT_HEADS, DK_BLK % ATT_HEADS, DV_BLK % ATT_HEADS, DG_BLK % ATT_HEADS) == (0, 0, 0, 0)
    do = pl.pallas_call(
        _diffattn_kernel,
        grid=(b, DIFF_HEADS // ATT_HEADS),
        in_specs=[
            pl.BlockSpec((None, s, hw), lambda bi, h: (bi, 0, DQ_BLK // ATT_HEADS + h)),
            pl.BlockSpec((None, s, hw), lambda bi, h: (bi, 0, DK_BLK // ATT_HEADS + h)),
            pl.BlockSpec((None, s, hw), lambda bi, h: (bi, 0, DV_BLK // ATT_HEADS + h)),
            pl.BlockSpec((None, s, hw), lambda bi, h: (bi, 0, DG_BLK // ATT_HEADS + h)),
            pl.BlockSpec((ATT_TK, 2 * ATT_TQ), lambda *_: (0, 0)),
            pl.BlockSpec((1, DIFF_DH), lambda *_: (0, 0)),
            pl.BlockSpec((1, DIFF_DH), lambda *_: (0, 0)),
            pl.BlockSpec((1, DIFF_DH), lambda *_: (0, 0)),
            pl.BlockSpec((1, DIFF_DH), lambda *_: (0, 0)),
            pl.BlockSpec((1, DIFF_DV), lambda *_: (0, 0)),
        ],
        out_specs=pl.BlockSpec((None, s, hw), lambda bi, h: (bi, 0, h)),
        out_shape=jax.ShapeDtypeStruct((b, s, DIFF_WIDTH), BF16),
        scratch_shapes=[pltpu.VMEM((ATT_HEADS, DIFF_DV, s), BF16)],
        compiler_params=params(dimension_semantics=("arbitrary", "arbitrary")),
        name="diffattn",
    )(z3, z3, z3, z3, diag_bias, vec(diff_lq1), vec(diff_lk1), vec(diff_lq2), vec(diff_lk2), vec(diff_subln))

    resident = functools.partial(pl.BlockSpec, pipeline_mode=pl.Buffered(1))
    out = pl.pallas_call(
        _out_kernel,
        grid=(m // OUT_TM,),
        in_specs=[
            pl.BlockSpec((OUT_TM, d), lambda i: (i, 0)),
            pl.BlockSpec((OUT_TM, RET_WIDTH), lambda i: (i, 0)),
            pl.BlockSpec((OUT_TM, DIFF_WIDTH), lambda i: (i, 0)),
            pl.BlockSpec((OUT_TM, PLE_DIM), lambda i: (i, 0)),
            resident((d, d), lambda i: (0, 0)),
            pl.BlockSpec((1, d), lambda i: (0, 0)),
            resident((d, d), lambda i: (0, 0)),
            resident((PLE_DIM, d), lambda i: (0, 0)),
        ],
        out_specs=pl.BlockSpec((OUT_TM, d), lambda i: (i, 0)),
        out_shape=jax.ShapeDtypeStruct((m, d), F32),
        compiler_params=params(dimension_semantics=("arbitrary",)),
        name="outproj",
    )(x2, ro.reshape(m, RET_WIDTH), do.reshape(m, DIFF_WIDTH), p2, w_out[0].astype(BF16),
      ple_norm[0].reshape(1, d), w_ple_gate[0].astype(BF16), w_ple_proj[0].astype(BF16))

    return out.reshape(b, s, d)
```

```python
import functools
import math

import jax
import jax.numpy as jnp
import numpy as np
from jax import lax
from jax.experimental import pallas as pl
from jax.experimental.pallas import tpu as pltpu

F32 = jnp.float32
BF16 = jnp.bfloat16

D_MODEL = 2048
CHUNK = 64
PLE_DIM = 256
RET_WIDTH = 1024
DIFF_WIDTH = 1024
RET_HEADS = 8
RET_DV = 128
RET_DK = 64
DIFF_HEADS = 8
DIFF_DV = 128
DIFF_DH = 64
ROPE_BASE = 10000.0
EPS = 1e-6
D_IN = 7168
LAM_INIT = 0.8 - 0.6 * math.exp(-0.3 * 0)

LANES = 128
VMEM_LIMIT_BYTES = 56 * 1024 * 1024

RQ_BLK = 0
RK_BLK = 512 // LANES
DQ_BLK = 3072 // LANES
DK_BLK = 4096 // LANES
DV_BLK = 5120 // LANES
DG_BLK = 6144 // LANES

PROJ_TM = 1024
PROJ_TN = 1024
ROW_CHUNK = 256
RET_T = 256
ATT_TQ = 256
ATT_TK = 256
ATT_HEADS = 2
OUT_TM = 512
OUT_CHUNK = 256
LOG2E = math.log2(math.e)
NEG_BIG = -1e30


def _silu(a):
    return a * (1.0 / (1.0 + jnp.exp(-a)))


def _swap_halves(a):
    lane = lax.broadcasted_iota(jnp.int32, a.shape, 1)
    first = (lane % RET_DK) < (RET_DK // 2)
    return jnp.where(first, pltpu.roll(a, LANES - RET_DK // 2, 1), pltpu.roll(a, RET_DK // 2, 1))


def _proj_kernel(x_ref, g_ref, w_ref, cos_ref, sin_ref, qn_ref, kn_ref, o_ref, u_ref, wb_ref):
    j = pl.program_id(1)
    tm = x_ref.shape[0]
    tn = w_ref.shape[1]

    def rope_epilogue(a, rows):
        c = cos_ref[rows, :]
        s = sin_ref[rows, :]
        for sl in range(tn // LANES):
            scale = 1.0 if sl * LANES < RET_HEADS * RET_DK else RET_DK ** -0.5
            asl = a[:, sl * LANES:(sl + 1) * LANES]
            o_ref[rows, pl.ds(sl * LANES, LANES)] = ((asl * c + _swap_halves(asl) * s) * scale).astype(BF16)

    def norm_epilogue(gn_ref, scale):
        def apply(a, rows):
            low = lax.broadcasted_iota(jnp.int32, (1, LANES), 1) < DIFF_DH
            for sl in range(tn // LANES):
                asl = a[:, sl * LANES:(sl + 1) * LANES]
                sq = asl * asl
                s_lo = jnp.sum(jnp.where(low, sq, 0.0), axis=-1, keepdims=True)
                s_hi = jnp.sum(jnp.where(low, 0.0, sq), axis=-1, keepdims=True)
                ms = jnp.where(low, s_lo, s_hi) * (1.0 / DIFF_DH)
                o_ref[rows, pl.ds(sl * LANES, LANES)] = (asl * lax.rsqrt(ms + EPS) * gn_ref[...] * scale).astype(BF16)
        return apply

    def silu_epilogue(a, rows):
        o_ref[rows, :] = _silu(a).astype(BF16)

    def plain_epilogue(a, rows):
        o_ref[rows, :] = a.astype(BF16)

    def run(epilogue, normalize_input=False):
        sizes = [ROW_CHUNK] * (tm // ROW_CHUNK - 1) + [ROW_CHUNK // 2] * 2
        for kt in range(w_ref.shape[0] // ROW_CHUNK):
            krows = pl.ds(kt * ROW_CHUNK, ROW_CHUNK)
            wb_ref[krows, :] = w_ref[krows, :].astype(BF16)
        for r, size in enumerate(sizes):
            rows = pl.ds(sum(sizes[:r]), size)
            if normalize_input:
                x = x_ref[rows, :]
                ms = jnp.mean(x * x, axis=-1, keepdims=True)
                u_ref[rows, :] = (x * lax.rsqrt(ms + EPS) * g_ref[...]).astype(BF16)
            epilogue(jnp.dot(u_ref[rows, :], wb_ref[...], preferred_element_type=F32), rows)

    @pl.when(j == 0)
    def _():
        run(rope_epilogue, normalize_input=True)

    @pl.when((j == 1) | (j == 5))
    def _():
        run(plain_epilogue)

    @pl.when((j == 2) | (j == 6))
    def _():
        run(silu_epilogue)

    @pl.when(j == 3)
    def _():
        run(norm_epilogue(qn_ref, DIFF_DH ** -0.5 * LOG2E))

    @pl.when(j == 4)
    def _():
        run(norm_epilogue(kn_ref, 1.0))


def _retention_kernel(q_ref, k_ref, v_ref, g_ref, gn_ref, dmat_ref, xi_ref, zeta_ref, gdec_ref, o_ref):
    s_len = q_ref.shape[0]
    lane = lax.broadcasted_iota(jnp.int32, (1, LANES), 1)

    def front(n):
        rows = pl.ds(n * RET_T, RET_T)
        q = q_ref[rows, :]
        k = k_ref[rows, :]
        vz = (v_ref[rows, :].astype(F32) * zeta_ref[...]).astype(BF16)
        kv = lax.dot_general(k, vz, (((0,), (0,)), ((), ())), preferred_element_type=F32)
        scs = []
        for hh in range(2):
            qm = jnp.where((lane < RET_DK) == (hh == 0), q, jnp.zeros_like(q))
            sc = lax.dot_general(qm, k, (((1,), (1,)), ((), ())), preferred_element_type=F32)
            scs.append((sc * dmat_ref[hh]).astype(BF16))
        return kv, scs

    def back(n, scs, state):
        rows = pl.ds(n * RET_T, RET_T)
        cross = jnp.dot(q_ref[rows, :], state.astype(BF16), preferred_element_type=F32)
        for hh in range(2):
            cols = pl.ds(hh * RET_DV, RET_DV)
            intra = jnp.dot(scs[hh], v_ref[rows, cols], preferred_element_type=F32)
            o = intra + cross[:, hh * RET_DV:(hh + 1) * RET_DV] * xi_ref[:, cols]
            ms = jnp.mean(o * o, axis=-1, keepdims=True)
            o = o * lax.rsqrt(ms + EPS) * gn_ref[:, cols]
            o_ref[rows, cols] = (o * g_ref[rows, cols].astype(F32)).astype(BF16)

    n_steps = s_len // RET_T
    state = jnp.zeros((2 * RET_DK, 2 * RET_DV), F32)
    cur = front(0)
    for n in range(n_steps):
        nxt = front(n + 1) if n + 1 < n_steps else None
        kv, scs = cur
        back(n, scs, state)
        state = state * gdec_ref[0] + kv * gdec_ref[1]
        cur = nxt


def _diffattn_kernel(q_ref, k_ref, v_ref, g_ref, bias_ref, lq1_ref, lk1_ref, lq2_ref, lk2_ref, sub_ref,
                     wo_ref, wg_ref, wp_ref, o_ref, wo_b_ref, wg_b_ref, wp_b_ref, vt_ref):
    wo_b_ref[...] = wo_ref[...].astype(BF16)
    wg_b_ref[...] = wg_ref[...].astype(BF16)
    wp_b_ref[...] = wp_ref[...].astype(BF16)
    s_len = k_ref.shape[0]
    tq = ATT_TQ
    for hh in range(ATT_HEADS):
        for t in range(s_len // ATT_TK):
            cols = pl.ds(t * ATT_TK, ATT_TK)
            vt_ref[hh, :, cols] = v_ref[cols, hh * DIFF_DV:(hh + 1) * DIFF_DV].astype(F32).T.astype(BF16)

    lam = (jnp.exp(jnp.sum(lq1_ref[...] * lk1_ref[...], axis=-1, keepdims=True))
           - jnp.exp(jnp.sum(lq2_ref[...] * lk2_ref[...], axis=-1, keepdims=True)) + LAM_INIT)

    def scores(hh, qi):
        lo = qi * tq
        hcols = slice(hh * LANES, (hh + 1) * LANES)
        qt = q_ref[lo:lo + tq, hcols].astype(F32).T
        row = lax.broadcasted_iota(jnp.int32, qt.shape, 0)
        qbd = jnp.concatenate([jnp.where(row < DIFF_DH, qt, 0.0), jnp.where(row >= DIFF_DH, qt, 0.0)],
                              axis=1).astype(BF16)
        sd = jnp.dot(k_ref[lo:lo + tq, hcols], qbd, preferred_element_type=F32) + bias_ref[...]
        m = jnp.max(sd, axis=0, keepdims=True)
        so = None
        if qi > 0:
            so = jnp.dot(k_ref[0:lo, hcols], qbd, preferred_element_type=F32)
            m = jnp.maximum(m, jnp.max(so, axis=0, keepdims=True))
        return sd, so, m

    def finish(hh, qi, sd, so, m):
        lo = qi * tq
        hcols = slice(hh * LANES, (hh + 1) * LANES)
        pd = jnp.exp2(sd - m)
        l = jnp.sum(pd, axis=0, keepdims=True)
        p = pd.astype(BF16)
        if qi > 0:
            po = jnp.exp2(so - m)
            l = l + jnp.sum(po, axis=0, keepdims=True)
            p = jnp.concatenate([po.astype(BF16), p], axis=0)
        on = jnp.dot(vt_ref[hh, :, 0:lo + tq], p, preferred_element_type=F32) / l
        o = (on[:, 0:tq] - lam * on[:, tq:2 * tq]).T
        ms_o = jnp.mean(o * o, axis=-1, keepdims=True)
        o = o * lax.rsqrt(ms_o + EPS) * sub_ref[...] * (1.0 - LAM_INIT)
        o_ref[lo:lo + tq, hcols] = (o * g_ref[lo:lo + tq, hcols].astype(F32)).astype(BF16)

    order = [(hh, qi) for hh in range(ATT_HEADS) for qi in reversed(range(s_len // tq))]
    cur = scores(*order[0])
    for n, unit in enumerate(order):
        nxt = scores(*order[n + 1]) if n + 1 < len(order) else None
        finish(*unit, *cur)
        cur = nxt


def _out_kernel(x_ref, ro_ref, do_ref, p_ref, wo_ref, pn_ref, wg_ref, wp_ref, o_ref):
    tm = x_ref.shape[0]

    def residual(rows):
        acc = jnp.dot(ro_ref[rows, :], wo_ref[0:RET_WIDTH, :], preferred_element_type=F32)
        acc = acc + jnp.dot(do_ref[rows, :], wo_ref[RET_WIDTH:, :], preferred_element_type=F32)
        h1 = x_ref[rows, :] + acc
        ms = jnp.mean(h1 * h1, axis=-1, keepdims=True)
        return h1, (h1 * lax.rsqrt(ms + EPS) * pn_ref[...]).astype(BF16)

    def gated(rows, h1, hn):
        z = jnp.dot(hn, wg_ref[...], preferred_element_type=F32)
        gate = 1.0 / (1.0 + jnp.exp(-z))
        ple = jnp.dot(p_ref[rows, :].astype(BF16), wp_ref[...], preferred_element_type=F32)
        o_ref[rows, :] = h1 + gate * ple

    chunks = [pl.ds(r * OUT_CHUNK, OUT_CHUNK) for r in range(tm // OUT_CHUNK)]
    cur = residual(chunks[0])
    for n, rows in enumerate(chunks):
        nxt = residual(chunks[n + 1]) if n + 1 < len(chunks) else None
        gated(rows, *cur)
        cur = nxt


def _decay_tables():
    log_g = np.log1p(-np.exp2(-5.0 - np.arange(RET_HEADS, dtype=np.float64)))
    idx = np.arange(RET_T, dtype=np.float64)
    dist = np.abs(idx[:, None] - idx[None, :])
    same_or_earlier = (np.arange(RET_T)[None, :] // CHUNK) <= (np.arange(RET_T)[:, None] // CHUNK)
    dmat = np.where(same_or_earlier[None], np.exp(dist[None] * log_g[:, None, None]), 0.0)
    xi = np.exp((idx + 1.0)[None, :] * log_g[:, None])
    zeta = np.exp((RET_T - 1.0 - idx)[None, :] * log_g[:, None])
    g_step = np.exp(RET_T * log_g)
    n_pair = RET_HEADS // 2
    widen = lambda t: np.broadcast_to(t.reshape(n_pair, 2, RET_T, 1), (n_pair, 2, RET_T, RET_DV)) \
        .transpose(0, 2, 1, 3).reshape(n_pair, RET_T, 2 * RET_DV)
    xi_w = widen(xi)
    zeta_w = widen(zeta)
    rowh = (np.arange(2 * RET_DK) // RET_DK)[:, None]
    colh = (np.arange(2 * RET_DV) // RET_DV)[None, :]
    diag = (rowh == colh).astype(np.float64)
    gp = g_step.reshape(n_pair, 2)
    gq = np.where(colh[None] == 0, gp[:, 0, None, None], gp[:, 1, None, None]) * diag[None]
    gdec = np.stack([gq, np.broadcast_to(diag[None], gq.shape)], axis=1)
    return tuple(jnp.asarray(t.astype(np.float32)) for t in (dmat, xi_w, zeta_w, gdec))


def _rope_tables(s_len):
    pos = np.arange(s_len, dtype=np.float64)
    inv_freq = ROPE_BASE ** (-np.arange(RET_DK // 2, dtype=np.float64) / (RET_DK // 2))
    ang = pos[:, None] * inv_freq[None, :]
    cos, sin = np.cos(ang), np.sin(ang)
    cos_t = np.tile(cos, (1, LANES // (RET_DK // 2)))
    sin_t = np.tile(np.concatenate([-sin, sin], axis=-1), (1, LANES // RET_DK))
    return jnp.asarray(cos_t.astype(np.float32)), jnp.asarray(sin_t.astype(np.float32))


def kernel(x, p, attn_norm, w_in, ret_gn, diff_qn, diff_kn, diff_lq1, diff_lk1, diff_lq2, diff_lk2, diff_subln, w_out, ple_norm, w_ple_gate, w_ple_proj):
    b, s, d = x.shape
    m = b * s
    assert d == D_MODEL and s % PROJ_TM == 0 and m % OUT_TM == 0 and s % RET_T == 0 and s % ATT_TQ == 0
    x2 = x.reshape(m, d)
    p2 = p[0].reshape(m, PLE_DIM)

    cos_t, sin_t = _rope_tables(s)
    dmat, xi_w, zeta_w, gdec = _decay_tables()
    qn_t = jnp.tile(diff_qn[0], LANES // DIFF_DH).reshape(1, LANES)
    kn_t = jnp.tile(diff_kn[0], LANES // DIFF_DH).reshape(1, LANES)

    params = functools.partial(pltpu.CompilerParams, vmem_limit_bytes=VMEM_LIMIT_BYTES)
    const = lambda *_: (0, 0)
    pos_blocks = s // PROJ_TM

    z = pl.pallas_call(
        _proj_kernel,
        grid=(m // PROJ_TM, D_IN // PROJ_TN),
        in_specs=[
            pl.BlockSpec((PROJ_TM, d), lambda i, j: (i, 0)),
            pl.BlockSpec((1, d), const),
            pl.BlockSpec((d, PROJ_TN), lambda i, j: (0, j)),
            pl.BlockSpec((PROJ_TM, LANES), lambda i, j: (i % pos_blocks, 0)),
            pl.BlockSpec((PROJ_TM, LANES), lambda i, j: (i % pos_blocks, 0)),
            pl.BlockSpec((1, LANES), const),
            pl.BlockSpec((1, LANES), const),
        ],
        out_specs=pl.BlockSpec((PROJ_TM, PROJ_TN), lambda i, j: (i, j)),
        out_shape=jax.ShapeDtypeStruct((m, D_IN), BF16),
        scratch_shapes=[pltpu.VMEM((PROJ_TM, d), BF16), pltpu.VMEM((d, PROJ_TN), BF16)],
        compiler_params=params(dimension_semantics=("arbitrary", "arbitrary")),
        name="proj",
    )(x2, attn_norm[0].reshape(1, d), w_in[0], cos_t, sin_t, qn_t, kn_t)

    z3 = z.reshape(b, s, D_IN)
    n_pair = RET_HEADS // 2
    ro = pl.pallas_call(
        _retention_kernel,
        grid=(b, n_pair),
        in_specs=[
            pl.BlockSpec((None, s, LANES), lambda bi, hp: (bi, 0, RQ_BLK + hp)),
            pl.BlockSpec((None, s, LANES), lambda bi, hp: (bi, 0, RK_BLK + hp)),
            pl.BlockSpec((None, s, 2 * RET_DV), lambda bi, hp: (bi, 0, 1024 // 256 + hp)),
            pl.BlockSpec((None, s, 2 * RET_DV), lambda bi, hp: (bi, 0, 2048 // 256 + hp)),
            pl.BlockSpec((1, 2 * RET_DV), lambda bi, hp: (0, hp)),
            pl.BlockSpec((2, RET_T, RET_T), lambda bi, hp: (hp, 0, 0)),
            pl.BlockSpec((None, RET_T, 2 * RET_DV), lambda bi, hp: (hp, 0, 0)),
            pl.BlockSpec((None, RET_T, 2 * RET_DV), lambda bi, hp: (hp, 0, 0)),
            pl.BlockSpec((None, 2, 2 * RET_DK, 2 * RET_DV), lambda bi, hp: (hp, 0, 0, 0)),
        ],
        out_specs=pl.BlockSpec((None, s, 2 * RET_DV), lambda bi, hp: (bi, 0, hp)),
        out_shape=jax.ShapeDtypeStruct((b, s, RET_WIDTH), BF16),
        compiler_params=params(dimension_semantics=("arbitrary", "arbitrary")),
        name="retention",
    )(z3, z3, z3, z3, ret_gn[0].reshape(1, RET_WIDTH), dmat, xi_w, zeta_w, gdec)

    vec = lambda a: a[0].reshape(1, -1)
    tile_chunk = np.arange(ATT_TK) // CHUNK
    diag_bias = np.where(tile_chunk[:, None] <= tile_chunk[None, :], 0.0, NEG_BIG).astype(np.float32)
    diag_bias = jnp.asarray(np.tile(diag_bias, (1, 2)))
    hw = ATT_HEADS * LANES
    assert (DQ_BLK % ATT_HEADS, DK_BLK % ATT_HEADS, DV_BLK % ATT_HEADS, DG_BLK % ATT_HEADS) == (0, 0, 0, 0)
    n_att = DIFF_HEADS // ATT_HEADS
    w_slice = lambda rows: pl.BlockSpec((rows // (b * n_att), d), lambda bi, h: (bi * n_att + h, 0))
    do, wo_b, wg_b, wp_b = pl.pallas_call(
        _diffattn_kernel,
        grid=(b, n_att),
        in_specs=[
            pl.BlockSpec((None, s, hw), lambda bi, h: (bi, 0, DQ_BLK // ATT_HEADS + h)),
            pl.BlockSpec((None, s, hw), lambda bi, h: (bi, 0, DK_BLK // ATT_HEADS + h)),
            pl.BlockSpec((None, s, hw), lambda bi, h: (bi, 0, DV_BLK // ATT_HEADS + h)),
            pl.BlockSpec((None, s, hw), lambda bi, h: (bi, 0, DG_BLK // ATT_HEADS + h)),
            pl.BlockSpec((ATT_TK, 2 * ATT_TQ), lambda *_: (0, 0)),
            pl.BlockSpec((1, DIFF_DH), lambda *_: (0, 0)),
            pl.BlockSpec((1, DIFF_DH), lambda *_: (0, 0)),
            pl.BlockSpec((1, DIFF_DH), lambda *_: (0, 0)),
            pl.BlockSpec((1, DIFF_DH), lambda *_: (0, 0)),
            pl.BlockSpec((1, DIFF_DV), lambda *_: (0, 0)),
            w_slice(d), w_slice(d), w_slice(PLE_DIM),
        ],
        out_specs=[pl.BlockSpec((None, s, hw), lambda bi, h: (bi, 0, h)),
                   w_slice(d), w_slice(d), w_slice(PLE_DIM)],
        out_shape=[jax.ShapeDtypeStruct((b, s, DIFF_WIDTH), BF16),
                   jax.ShapeDtypeStruct((d, d), BF16), jax.ShapeDtypeStruct((d, d), BF16),
                   jax.ShapeDtypeStruct((PLE_DIM, d), BF16)],
        scratch_shapes=[pltpu.VMEM((ATT_HEADS, DIFF_DV, s), BF16)],
        compiler_params=params(dimension_semantics=("arbitrary", "arbitrary")),
        name="diffattn",
    )(z3, z3, z3, z3, diag_bias, vec(diff_lq1), vec(diff_lk1), vec(diff_lq2), vec(diff_lk2), vec(diff_subln),
      w_out[0], w_ple_gate[0], w_ple_proj[0])

    resident = functools.partial(pl.BlockSpec, pipeline_mode=pl.Buffered(1))
    out = pl.pallas_call(
        _out_kernel,
        grid=(m // OUT_TM,),
        in_specs=[
            pl.BlockSpec((OUT_TM, d), lambda i: (i, 0)),
            pl.BlockSpec((OUT_TM, RET_WIDTH), lambda i: (i, 0)),
            pl.BlockSpec((OUT_TM, DIFF_WIDTH), lambda i: (i, 0)),
            pl.BlockSpec((OUT_TM, PLE_DIM), lambda i: (i, 0)),
            resident((d, d), lambda i: (0, 0)),
            pl.BlockSpec((1, d), lambda i: (0, 0)),
            resident((d, d), lambda i: (0, 0)),
            resident((PLE_DIM, d), lambda i: (0, 0)),
        ],
        out_specs=pl.BlockSpec((OUT_TM, d), lambda i: (i, 0)),
        out_shape=jax.ShapeDtypeStruct((m, d), F32),
        compiler_params=params(dimension_semantics=("arbitrary",)),
        name="outproj",
    )(x2, ro.reshape(m, RET_WIDTH), do.reshape(m, DIFF_WIDTH), p2, wo_b, ple_norm[0].reshape(1, d), wg_b, wp_b)

    return out.reshape(b, s, d)
```

```python
import functools
import math

import jax
import jax.numpy as jnp
import numpy as np
from jax import lax
from jax.experimental import pallas as pl
from jax.experimental.pallas import tpu as pltpu

F32 = jnp.float32
BF16 = jnp.bfloat16

D_MODEL = 2048
CHUNK = 64
PLE_DIM = 256
RET_WIDTH = 1024
DIFF_WIDTH = 1024
RET_HEADS = 8
RET_DV = 128
RET_DK = 64
DIFF_HEADS = 8
DIFF_DV = 128
DIFF_DH = 64
ROPE_BASE = 10000.0
EPS = 1e-6
D_IN = 7168
LAM_INIT = 0.8 - 0.6 * math.exp(-0.3 * 0)

LANES = 128
VMEM_LIMIT_BYTES = 56 * 1024 * 1024

RQ_BLK = 0
RK_BLK = 512 // LANES
DQ_BLK = 3072 // LANES
DK_BLK = 4096 // LANES
DV_BLK = 5120 // LANES
DG_BLK = 6144 // LANES

PROJ_TM = 1024
PROJ_TN = 1024
ROW_CHUNK = 256
RET_T = 256
ATT_TQ = 256
ATT_TK = 256
ATT_HEADS = 2
OUT_TM = 512
OUT_CHUNK = 256
LOG2E = math.log2(math.e)
NEG_BIG = -1e30


def _silu(a):
    return a * (1.0 / (1.0 + jnp.exp(-a)))


def _swap_halves(a):
    lane = lax.broadcasted_iota(jnp.int32, a.shape, 1)
    first = (lane % RET_DK) < (RET_DK // 2)
    return jnp.where(first, pltpu.roll(a, LANES - RET_DK // 2, 1), pltpu.roll(a, RET_DK // 2, 1))


def _proj_kernel(x_ref, g_ref, w_ref, cos_ref, sin_ref, qn_ref, kn_ref, o_ref, u_ref, wb_ref):
    j = pl.program_id(1)
    tm = x_ref.shape[0]
    tn = w_ref.shape[1]

    def rope_epilogue(a, rows):
        c = cos_ref[rows, :]
        s = sin_ref[rows, :]
        for sl in range(tn // LANES):
            scale = 1.0 if sl * LANES < RET_HEADS * RET_DK else RET_DK ** -0.5
            asl = a[:, sl * LANES:(sl + 1) * LANES]
            o_ref[rows, pl.ds(sl * LANES, LANES)] = ((asl * c + _swap_halves(asl) * s) * scale).astype(BF16)

    def norm_epilogue(gn_ref, scale):
        def apply(a, rows):
            low = lax.broadcasted_iota(jnp.int32, (1, LANES), 1) < DIFF_DH
            for sl in range(tn // LANES):
                asl = a[:, sl * LANES:(sl + 1) * LANES]
                sq = asl * asl
                s_lo = jnp.sum(jnp.where(low, sq, 0.0), axis=-1, keepdims=True)
                s_hi = jnp.sum(jnp.where(low, 0.0, sq), axis=-1, keepdims=True)
                ms = jnp.where(low, s_lo, s_hi) * (1.0 / DIFF_DH)
                o_ref[rows, pl.ds(sl * LANES, LANES)] = (asl * lax.rsqrt(ms + EPS) * gn_ref[...] * scale).astype(BF16)
        return apply

    def silu_epilogue(a, rows):
        o_ref[rows, :] = _silu(a).astype(BF16)

    def plain_epilogue(a, rows):
        o_ref[rows, :] = a.astype(BF16)

    def run(epilogue, normalize_input=False):
        sizes = [ROW_CHUNK] * (tm // ROW_CHUNK - 1) + [ROW_CHUNK // 2] * 2
        for kt in range(w_ref.shape[0] // ROW_CHUNK):
            krows = pl.ds(kt * ROW_CHUNK, ROW_CHUNK)
            wb_ref[krows, :] = w_ref[krows, :].astype(BF16)
        for r, size in enumerate(sizes):
            rows = pl.ds(sum(sizes[:r]), size)
            if normalize_input:
                x = x_ref[rows, :]
                ms = jnp.mean(x * x, axis=-1, keepdims=True)
                u_ref[rows, :] = (x * lax.rsqrt(ms + EPS) * g_ref[...]).astype(BF16)
            epilogue(jnp.dot(u_ref[rows, :], wb_ref[...], preferred_element_type=F32), rows)

    @pl.when(j == 0)
    def _():
        run(rope_epilogue, normalize_input=True)

    @pl.when((j == 1) | (j == 5))
    def _():
        run(plain_epilogue)

    @pl.when((j == 2) | (j == 6))
    def _():
        run(silu_epilogue)

    @pl.when(j == 3)
    def _():
        run(norm_epilogue(qn_ref, DIFF_DH ** -0.5 * LOG2E))

    @pl.when(j == 4)
    def _():
        run(norm_epilogue(kn_ref, 1.0))


def _retention_stages(q_ref, k_ref, v_ref, g_ref, gn_ref, dmat_ref, xi_ref, zeta_ref, gdec_ref, o_ref):
    s_len = q_ref.shape[0]
    lane = lax.broadcasted_iota(jnp.int32, (1, LANES), 1)

    def front(n):
        rows = pl.ds(n * RET_T, RET_T)
        q = q_ref[rows, :]
        k = k_ref[rows, :]
        vz = (v_ref[rows, :].astype(F32) * zeta_ref[...]).astype(BF16)
        kv = lax.dot_general(k, vz, (((0,), (0,)), ((), ())), preferred_element_type=F32)
        scs = []
        for hh in range(2):
            qm = jnp.where((lane < RET_DK) == (hh == 0), q, jnp.zeros_like(q))
            sc = lax.dot_general(qm, k, (((1,), (1,)), ((), ())), preferred_element_type=F32)
            scs.append((sc * dmat_ref[hh]).astype(BF16))
        return kv, scs

    def back(n, scs, state):
        rows = pl.ds(n * RET_T, RET_T)
        cross = jnp.dot(q_ref[rows, :], state.astype(BF16), preferred_element_type=F32)
        for hh in range(2):
            cols = pl.ds(hh * RET_DV, RET_DV)
            intra = jnp.dot(scs[hh], v_ref[rows, cols], preferred_element_type=F32)
            o = intra + cross[:, hh * RET_DV:(hh + 1) * RET_DV] * xi_ref[:, cols]
            ms = jnp.mean(o * o, axis=-1, keepdims=True)
            o = o * lax.rsqrt(ms + EPS) * gn_ref[:, cols]
            o_ref[rows, cols] = (o * g_ref[rows, cols].astype(F32)).astype(BF16)

    def advance(state, kv):
        return state * gdec_ref[0] + kv * gdec_ref[1]

    return front, back, advance, s_len // RET_T


def _attention_stages(q_ref, k_ref, v_ref, g_ref, bias_ref, lq1_ref, lk1_ref, lq2_ref, lk2_ref, sub_ref, o_ref,
                      vt_ref):
    s_len = k_ref.shape[0]
    tq = ATT_TQ
    for hh in range(ATT_HEADS):
        for t in range(s_len // ATT_TK):
            cols = pl.ds(t * ATT_TK, ATT_TK)
            vt_ref[hh, :, cols] = v_ref[cols, hh * DIFF_DV:(hh + 1) * DIFF_DV].astype(F32).T.astype(BF16)

    lam = (jnp.exp(jnp.sum(lq1_ref[...] * lk1_ref[...], axis=-1, keepdims=True))
           - jnp.exp(jnp.sum(lq2_ref[...] * lk2_ref[...], axis=-1, keepdims=True)) + LAM_INIT)

    def scores(hh, qi):
        lo = qi * tq
        hcols = slice(hh * LANES, (hh + 1) * LANES)
        qt = q_ref[lo:lo + tq, hcols].astype(F32).T
        row = lax.broadcasted_iota(jnp.int32, qt.shape, 0)
        qbd = jnp.concatenate([jnp.where(row < DIFF_DH, qt, 0.0), jnp.where(row >= DIFF_DH, qt, 0.0)],
                              axis=1).astype(BF16)
        sd = jnp.dot(k_ref[lo:lo + tq, hcols], qbd, preferred_element_type=F32) + bias_ref[...]
        m = jnp.max(sd, axis=0, keepdims=True)
        so = None
        if qi > 0:
            so = jnp.dot(k_ref[0:lo, hcols], qbd, preferred_element_type=F32)
            m = jnp.maximum(m, jnp.max(so, axis=0, keepdims=True))
        return sd, so, m

    def finish(hh, qi, sd, so, m):
        lo = qi * tq
        hcols = slice(hh * LANES, (hh + 1) * LANES)
        pd = jnp.exp2(sd - m)
        l = jnp.sum(pd, axis=0, keepdims=True)
        p = pd.astype(BF16)
        if qi > 0:
            po = jnp.exp2(so - m)
            l = l + jnp.sum(po, axis=0, keepdims=True)
            p = jnp.concatenate([po.astype(BF16), p], axis=0)
        on = jnp.dot(vt_ref[hh, :, 0:lo + tq], p, preferred_element_type=F32) / l
        o = (on[:, 0:tq] - lam * on[:, tq:2 * tq]).T
        ms_o = jnp.mean(o * o, axis=-1, keepdims=True)
        o = o * lax.rsqrt(ms_o + EPS) * sub_ref[...] * (1.0 - LAM_INIT)
        o_ref[lo:lo + tq, hcols] = (o * g_ref[lo:lo + tq, hcols].astype(F32)).astype(BF16)

    order = [(hh, qi) for hh in range(ATT_HEADS) for qi in reversed(range(s_len // tq))]
    return scores, finish, order


def _mixer_kernel(rq_ref, rk_ref, rv_ref, rg_ref, gn_ref, dmat_ref, xi_ref, zeta_ref, gdec_ref,
                  q_ref, k_ref, v_ref, g_ref, bias_ref, lq1_ref, lk1_ref, lq2_ref, lk2_ref, sub_ref,
                  wo_ref, wg_ref, wp_ref,
                  ro_ref, do_ref, wo_b_ref, wg_b_ref, wp_b_ref, vt_ref):
    wo_b_ref[...] = wo_ref[...].astype(BF16)
    wg_b_ref[...] = wg_ref[...].astype(BF16)
    wp_b_ref[...] = wp_ref[...].astype(BF16)

    r_front, r_back, r_advance, r_steps = _retention_stages(
        rq_ref, rk_ref, rv_ref, rg_ref, gn_ref, dmat_ref, xi_ref, zeta_ref, gdec_ref, ro_ref)
    a_scores, a_finish, order = _attention_stages(
        q_ref, k_ref, v_ref, g_ref, bias_ref, lq1_ref, lk1_ref, lq2_ref, lk2_ref, sub_ref, do_ref, vt_ref)
    every = len(order) // r_steps

    state = jnp.zeros((2 * RET_DK, 2 * RET_DV), F32)
    r_cur = r_front(0)
    a_cur = a_scores(*order[0])
    for n, unit in enumerate(order):
        a_nxt = a_scores(*order[n + 1]) if n + 1 < len(order) else None
        if n % every == 0:
            r_n = n // every
            r_nxt = r_front(r_n + 1) if r_n + 1 < r_steps else None
            kv, scs = r_cur
            r_back(r_n, scs, state)
            state = r_advance(state, kv)
            r_cur = r_nxt
        a_finish(*unit, *a_cur)
        a_cur = a_nxt


def _out_kernel(x_ref, ro_ref, do_ref, p_ref, wo_ref, pn_ref, wg_ref, wp_ref, o_ref):
    tm = x_ref.shape[0]

    def residual(rows):
        acc = jnp.dot(ro_ref[rows, :], wo_ref[0:RET_WIDTH, :], preferred_element_type=F32)
        acc = acc + jnp.dot(do_ref[rows, :], wo_ref[RET_WIDTH:, :], preferred_element_type=F32)
        h1 = x_ref[rows, :] + acc
        ms = jnp.mean(h1 * h1, axis=-1, keepdims=True)
        return h1, (h1 * lax.rsqrt(ms + EPS) * pn_ref[...]).astype(BF16)

    def gated(rows, h1, hn):
        z = jnp.dot(hn, wg_ref[...], preferred_element_type=F32)
        gate = 1.0 / (1.0 + jnp.exp(-z))
        ple = jnp.dot(p_ref[rows, :].astype(BF16), wp_ref[...], preferred_element_type=F32)
        o_ref[rows, :] = h1 + gate * ple

    chunks = [pl.ds(r * OUT_CHUNK, OUT_CHUNK) for r in range(tm // OUT_CHUNK)]
    cur = residual(chunks[0])
    for n, rows in enumerate(chunks):
        nxt = residual(chunks[n + 1]) if n + 1 < len(chunks) else None
        gated(rows, *cur)
        cur = nxt


def _decay_tables():
    log_g = np.log1p(-np.exp2(-5.0 - np.arange(RET_HEADS, dtype=np.float64)))
    idx = np.arange(RET_T, dtype=np.float64)
    dist = np.abs(idx[:, None] - idx[None, :])
    same_or_earlier = (np.arange(RET_T)[None, :] // CHUNK) <= (np.arange(RET_T)[:, None] // CHUNK)
    dmat = np.where(same_or_earlier[None], np.exp(dist[None] * log_g[:, None, None]), 0.0)
    xi = np.exp((idx + 1.0)[None, :] * log_g[:, None])
    zeta = np.exp((RET_T - 1.0 - idx)[None, :] * log_g[:, None])
    g_step = np.exp(RET_T * log_g)
    n_pair = RET_HEADS // 2
    widen = lambda t: np.broadcast_to(t.reshape(n_pair, 2, RET_T, 1), (n_pair, 2, RET_T, RET_DV)) \
        .transpose(0, 2, 1, 3).reshape(n_pair, RET_T, 2 * RET_DV)
    xi_w = widen(xi)
    zeta_w = widen(zeta)
    rowh = (np.arange(2 * RET_DK) // RET_DK)[:, None]
    colh = (np.arange(2 * RET_DV) // RET_DV)[None, :]
    diag = (rowh == colh).astype(np.float64)
    gp = g_step.reshape(n_pair, 2)
    gq = np.where(colh[None] == 0, gp[:, 0, None, None], gp[:, 1, None, None]) * diag[None]
    gdec = np.stack([gq, np.broadcast_to(diag[None], gq.shape)], axis=1)
    return tuple(jnp.asarray(t.astype(np.float32)) for t in (dmat, xi_w, zeta_w, gdec))


def _rope_tables(s_len):
    pos = np.arange(s_len, dtype=np.float64)
    inv_freq = ROPE_BASE ** (-np.arange(RET_DK // 2, dtype=np.float64) / (RET_DK // 2))
    ang = pos[:, None] * inv_freq[None, :]
    cos, sin = np.cos(ang), np.sin(ang)
    cos_t = np.tile(cos, (1, LANES // (RET_DK // 2)))
    sin_t = np.tile(np.concatenate([-sin, sin], axis=-1), (1, LANES // RET_DK))
    return jnp.asarray(cos_t.astype(np.float32)), jnp.asarray(sin_t.astype(np.float32))


def kernel(x, p, attn_norm, w_in, ret_gn, diff_qn, diff_kn, diff_lq1, diff_lk1, diff_lq2, diff_lk2, diff_subln, w_out, ple_norm, w_ple_gate, w_ple_proj):
    b, s, d = x.shape
    m = b * s
    assert d == D_MODEL and s % PROJ_TM == 0 and m % OUT_TM == 0 and s % RET_T == 0 and s % ATT_TQ == 0
    x2 = x.reshape(m, d)
    p2 = p[0].reshape(m, PLE_DIM)

    cos_t, sin_t = _rope_tables(s)
    dmat, xi_w, zeta_w, gdec = _decay_tables()
    qn_t = jnp.tile(diff_qn[0], LANES // DIFF_DH).reshape(1, LANES)
    kn_t = jnp.tile(diff_kn[0], LANES // DIFF_DH).reshape(1, LANES)

    params = functools.partial(pltpu.CompilerParams, vmem_limit_bytes=VMEM_LIMIT_BYTES)
    const = lambda *_: (0, 0)
    pos_blocks = s // PROJ_TM

    z = pl.pallas_call(
        _proj_kernel,
        grid=(m // PROJ_TM, D_IN // PROJ_TN),
        in_specs=[
            pl.BlockSpec((PROJ_TM, d), lambda i, j: (i, 0)),
            pl.BlockSpec((1, d), const),
            pl.BlockSpec((d, PROJ_TN), lambda i, j: (0, j)),
            pl.BlockSpec((PROJ_TM, LANES), lambda i, j: (i % pos_blocks, 0)),
            pl.BlockSpec((PROJ_TM, LANES), lambda i, j: (i % pos_blocks, 0)),
            pl.BlockSpec((1, LANES), const),
            pl.BlockSpec((1, LANES), const),
        ],
        out_specs=pl.BlockSpec((PROJ_TM, PROJ_TN), lambda i, j: (i, j)),
        out_shape=jax.ShapeDtypeStruct((m, D_IN), BF16),
        scratch_shapes=[pltpu.VMEM((PROJ_TM, d), BF16), pltpu.VMEM((d, PROJ_TN), BF16)],
        compiler_params=params(dimension_semantics=("arbitrary", "arbitrary")),
        name="proj",
    )(x2, attn_norm[0].reshape(1, d), w_in[0], cos_t, sin_t, qn_t, kn_t)

    z3 = z.reshape(b, s, D_IN)
    vec = lambda a: a[0].reshape(1, -1)
    tile_chunk = np.arange(ATT_TK) // CHUNK
    diag_bias = np.where(tile_chunk[:, None] <= tile_chunk[None, :], 0.0, NEG_BIG).astype(np.float32)
    diag_bias = jnp.asarray(np.tile(diag_bias, (1, 2)))
    hw = ATT_HEADS * LANES
    assert (DQ_BLK % ATT_HEADS, DK_BLK % ATT_HEADS, DV_BLK % ATT_HEADS, DG_BLK % ATT_HEADS) == (0, 0, 0, 0)
    n_att = DIFF_HEADS // ATT_HEADS
    assert n_att == RET_HEADS // 2
    w_slice = lambda rows: pl.BlockSpec((rows // (b * n_att), d), lambda bi, h: (bi * n_att + h, 0))
    ro, do, wo_b, wg_b, wp_b = pl.pallas_call(
        _mixer_kernel,
        grid=(b, n_att),
        in_specs=[
            pl.BlockSpec((None, s, LANES), lambda bi, hp: (bi, 0, RQ_BLK + hp)),
            pl.BlockSpec((None, s, LANES), lambda bi, hp: (bi, 0, RK_BLK + hp)),
            pl.BlockSpec((None, s, 2 * RET_DV), lambda bi, hp: (bi, 0, 1024 // 256 + hp)),
            pl.BlockSpec((None, s, 2 * RET_DV), lambda bi, hp: (bi, 0, 2048 // 256 + hp)),
            pl.BlockSpec((1, 2 * RET_DV), lambda bi, hp: (0, hp)),
            pl.BlockSpec((2, RET_T, RET_T), lambda bi, hp: (hp, 0, 0)),
            pl.BlockSpec((None, RET_T, 2 * RET_DV), lambda bi, hp: (hp, 0, 0)),
            pl.BlockSpec((None, RET_T, 2 * RET_DV), lambda bi, hp: (hp, 0, 0)),
            pl.BlockSpec((None, 2, 2 * RET_DK, 2 * RET_DV), lambda bi, hp: (hp, 0, 0, 0)),
            pl.BlockSpec((None, s, hw), lambda bi, h: (bi, 0, DQ_BLK // ATT_HEADS + h)),
            pl.BlockSpec((None, s, hw), lambda bi, h: (bi, 0, DK_BLK // ATT_HEADS + h)),
            pl.BlockSpec((None, s, hw), lambda bi, h: (bi, 0, DV_BLK // ATT_HEADS + h)),
            pl.BlockSpec((None, s, hw), lambda bi, h: (bi, 0, DG_BLK // ATT_HEADS + h)),
            pl.BlockSpec((ATT_TK, 2 * ATT_TQ), lambda *_: (0, 0)),
            pl.BlockSpec((1, DIFF_DH), lambda *_: (0, 0)),
            pl.BlockSpec((1, DIFF_DH), lambda *_: (0, 0)),
            pl.BlockSpec((1, DIFF_DH), lambda *_: (0, 0)),
            pl.BlockSpec((1, DIFF_DH), lambda *_: (0, 0)),
            pl.BlockSpec((1, DIFF_DV), lambda *_: (0, 0)),
            w_slice(d), w_slice(d), w_slice(PLE_DIM),
        ],
        out_specs=[pl.BlockSpec((None, s, 2 * RET_DV), lambda bi, hp: (bi, 0, hp)),
                   pl.BlockSpec((None, s, hw), lambda bi, h: (bi, 0, h)),
                   w_slice(d), w_slice(d), w_slice(PLE_DIM)],
        out_shape=[jax.ShapeDtypeStruct((b, s, RET_WIDTH), BF16),
                   jax.ShapeDtypeStruct((b, s, DIFF_WIDTH), BF16),
                   jax.ShapeDtypeStruct((d, d), BF16), jax.ShapeDtypeStruct((d, d), BF16),
                   jax.ShapeDtypeStruct((PLE_DIM, d), BF16)],
        scratch_shapes=[pltpu.VMEM((ATT_HEADS, DIFF_DV, s), BF16)],
        compiler_params=params(dimension_semantics=("arbitrary", "arbitrary")),
        name="mixer",
    )(z3, z3, z3, z3, ret_gn[0].reshape(1, RET_WIDTH), dmat, xi_w, zeta_w, gdec,
      z3, z3, z3, z3, diag_bias, vec(diff_lq1), vec(diff_lk1), vec(diff_lq2), vec(diff_lk2), vec(diff_subln),
      w_out[0], w_ple_gate[0], w_ple_proj[0])

    resident = functools.partial(pl.BlockSpec, pipeline_mode=pl.Buffered(1))
    out = pl.pallas_call(
        _out_kernel,
        grid=(m // OUT_TM,),
        in_specs=[
            pl.BlockSpec((OUT_TM, d), lambda i: (i, 0)),
            pl.BlockSpec((OUT_TM, RET_WIDTH), lambda i: (i, 0)),
            pl.BlockSpec((OUT_TM, DIFF_WIDTH), lambda i: (i, 0)),
            pl.BlockSpec((OUT_TM, PLE_DIM), lambda i: (i, 0)),
            resident((d, d), lambda i: (0, 0)),
            pl.BlockSpec((1, d), lambda i: (0, 0)),
            resident((d, d), lambda i: (0, 0)),
            resident((PLE_DIM, d), lambda i: (0, 0)),
        ],
        out_specs=pl.BlockSpec((OUT_TM, d), lambda i: (i, 0)),
        out_shape=jax.ShapeDtypeStruct((m, d), F32),
        compiler_params=params(dimension_semantics=("arbitrary",)),
        name="outproj",
    )(x2, ro.reshape(m, RET_WIDTH), do.reshape(m, DIFF_WIDTH), p2, wo_b, ple_norm[0].reshape(1, d), wg_b, wp_b)

    return out.reshape(b, s, d)
```

```python
import functools
import math

import jax
import jax.numpy as jnp
import numpy as np
from jax import lax
from jax.experimental import pallas as pl
from jax.experimental.pallas import tpu as pltpu

F32 = jnp.float32
BF16 = jnp.bfloat16

D_MODEL = 2048
CHUNK = 64
PLE_DIM = 256
RET_WIDTH = 1024
DIFF_WIDTH = 1024
RET_HEADS = 8
RET_DV = 128
RET_DK = 64
DIFF_HEADS = 8
DIFF_DV = 128
DIFF_DH = 64
ROPE_BASE = 10000.0
EPS = 1e-6
D_IN = 7168
LAM_INIT = 0.8 - 0.6 * math.exp(-0.3 * 0)

LANES = 128
VMEM_LIMIT_BYTES = 56 * 1024 * 1024

RQ_BLK = 0
RK_BLK = 512 // LANES
DQ_BLK = 3072 // LANES
DK_BLK = 4096 // LANES
DV_BLK = 5120 // LANES
DG_BLK = 6144 // LANES

PROJ_TM = 1024
PROJ_TN = 1024
ROW_CHUNK = 256
RET_T = 256
ATT_TQ = 256
ATT_TK = 256
ATT_HEADS = 2
OUT_TM = 512
OUT_CHUNK = 256
LOG2E = math.log2(math.e)
NEG_BIG = -1e30


def _silu(a):
    return a * (1.0 / (1.0 + jnp.exp(-a)))


def _swap_halves(a):
    lane = lax.broadcasted_iota(jnp.int32, a.shape, 1)
    first = (lane % RET_DK) < (RET_DK // 2)
    return jnp.where(first, pltpu.roll(a, LANES - RET_DK // 2, 1), pltpu.roll(a, RET_DK // 2, 1))


def _proj_first_kernel(x_ref, g_ref, w32_ref, cos_ref, sin_ref, qn_ref, kn_ref, o_ref, wb_ref, u_ref):
    _proj_body(x_ref, g_ref, wb_ref, cos_ref, sin_ref, qn_ref, kn_ref, o_ref, u_ref, w32_ref)


def _proj_rest_kernel(x_ref, g_ref, wb_ref, cos_ref, sin_ref, qn_ref, kn_ref, z0_ref, o_ref, u_ref):
    i = pl.program_id(0)

    @pl.when(i == 0)
    def _():
        o_ref[...] = z0_ref[...]

    @pl.when(i > 0)
    def _():
        _proj_body(x_ref, g_ref, wb_ref, cos_ref, sin_ref, qn_ref, kn_ref, o_ref, u_ref, None)


def _proj_body(x_ref, g_ref, wb_ref, cos_ref, sin_ref, qn_ref, kn_ref, o_ref, u_ref, w32_ref):
    j = pl.program_id(1)
    tm = x_ref.shape[0]
    tn = wb_ref.shape[1]

    def rope_epilogue(a, rows):
        c = cos_ref[rows, :]
        s = sin_ref[rows, :]
        for sl in range(tn // LANES):
            scale = 1.0 if sl * LANES < RET_HEADS * RET_DK else RET_DK ** -0.5
            asl = a[:, sl * LANES:(sl + 1) * LANES]
            o_ref[rows, pl.ds(sl * LANES, LANES)] = ((asl * c + _swap_halves(asl) * s) * scale).astype(BF16)

    def norm_epilogue(gn_ref, scale):
        def apply(a, rows):
            low = lax.broadcasted_iota(jnp.int32, (1, LANES), 1) < DIFF_DH
            for sl in range(tn // LANES):
                asl = a[:, sl * LANES:(sl + 1) * LANES]
                sq = asl * asl
                s_lo = jnp.sum(jnp.where(low, sq, 0.0), axis=-1, keepdims=True)
                s_hi = jnp.sum(jnp.where(low, 0.0, sq), axis=-1, keepdims=True)
                ms = jnp.where(low, s_lo, s_hi) * (1.0 / DIFF_DH)
                o_ref[rows, pl.ds(sl * LANES, LANES)] = (asl * lax.rsqrt(ms + EPS) * gn_ref[...] * scale).astype(BF16)
        return apply

    def silu_epilogue(a, rows):
        o_ref[rows, :] = _silu(a).astype(BF16)

    def plain_epilogue(a, rows):
        o_ref[rows, :] = a.astype(BF16)

    def run(epilogue, normalize_input=False):
        sizes = [ROW_CHUNK] * (tm // ROW_CHUNK - 1) + [ROW_CHUNK // 2] * 2
        if w32_ref is not None:
            for kt in range(w32_ref.shape[0] // ROW_CHUNK):
                krows = pl.ds(kt * ROW_CHUNK, ROW_CHUNK)
                wb_ref[krows, :] = w32_ref[krows, :].astype(BF16)
        for r, size in enumerate(sizes):
            rows = pl.ds(sum(sizes[:r]), size)
            if normalize_input:
                x = x_ref[rows, :]
                ms = jnp.mean(x * x, axis=-1, keepdims=True)
                u_ref[rows, :] = (x * lax.rsqrt(ms + EPS) * g_ref[...]).astype(BF16)
            epilogue(jnp.dot(u_ref[rows, :], wb_ref[...], preferred_element_type=F32), rows)

    @pl.when(j == 0)
    def _():
        run(rope_epilogue, normalize_input=True)

    @pl.when((j == 1) | (j == 5))
    def _():
        run(plain_epilogue)

    @pl.when((j == 2) | (j == 6))
    def _():
        run(silu_epilogue)

    @pl.when(j == 3)
    def _():
        run(norm_epilogue(qn_ref, DIFF_DH ** -0.5 * LOG2E))

    @pl.when(j == 4)
    def _():
        run(norm_epilogue(kn_ref, 1.0))


def _retention_stages(q_ref, k_ref, v_ref, g_ref, gn_ref, dmat_ref, xi_ref, zeta_ref, gdec_ref, o_ref):
    s_len = q_ref.shape[0]
    lane = lax.broadcasted_iota(jnp.int32, (1, LANES), 1)

    def front(n):
        rows = pl.ds(n * RET_T, RET_T)
        q = q_ref[rows, :]
        k = k_ref[rows, :]
        vz = (v_ref[rows, :].astype(F32) * zeta_ref[...]).astype(BF16)
        kv = lax.dot_general(k, vz, (((0,), (0,)), ((), ())), preferred_element_type=F32)
        scs = []
        for hh in range(2):
            qm = jnp.where((lane < RET_DK) == (hh == 0), q, jnp.zeros_like(q))
            sc = lax.dot_general(qm, k, (((1,), (1,)), ((), ())), preferred_element_type=F32)
            scs.append((sc * dmat_ref[hh]).astype(BF16))
        return kv, scs

    def back(n, scs, state):
        rows = pl.ds(n * RET_T, RET_T)
        cross = jnp.dot(q_ref[rows, :], state.astype(BF16), preferred_element_type=F32)
        for hh in range(2):
            cols = pl.ds(hh * RET_DV, RET_DV)
            intra = jnp.dot(scs[hh], v_ref[rows, cols], preferred_element_type=F32)
            o = intra + cross[:, hh * RET_DV:(hh + 1) * RET_DV] * xi_ref[:, cols]
            ms = jnp.mean(o * o, axis=-1, keepdims=True)
            o = o * lax.rsqrt(ms + EPS) * gn_ref[:, cols]
            o_ref[rows, cols] = (o * g_ref[rows, cols].astype(F32)).astype(BF16)

    def advance(state, kv):
        return state * gdec_ref[0] + kv * gdec_ref[1]

    return front, back, advance, s_len // RET_T


def _attention_stages(q_ref, k_ref, v_ref, g_ref, bias_ref, lq1_ref, lk1_ref, lq2_ref, lk2_ref, sub_ref, o_ref,
                      vt_ref):
    s_len = k_ref.shape[0]
    tq = ATT_TQ
    for hh in range(ATT_HEADS):
        for t in range(s_len // ATT_TK):
            cols = pl.ds(t * ATT_TK, ATT_TK)
            vt_ref[hh, :, cols] = v_ref[cols, hh * DIFF_DV:(hh + 1) * DIFF_DV].astype(F32).T.astype(BF16)

    lam = (jnp.exp(jnp.sum(lq1_ref[...] * lk1_ref[...], axis=-1, keepdims=True))
           - jnp.exp(jnp.sum(lq2_ref[...] * lk2_ref[...], axis=-1, keepdims=True)) + LAM_INIT)

    def scores(hh, qi):
        lo = qi * tq
        hcols = slice(hh * LANES, (hh + 1) * LANES)
        qt = q_ref[lo:lo + tq, hcols].astype(F32).T
        row = lax.broadcasted_iota(jnp.int32, qt.shape, 0)
        qbd = jnp.concatenate([jnp.where(row < DIFF_DH, qt, 0.0), jnp.where(row >= DIFF_DH, qt, 0.0)],
                              axis=1).astype(BF16)
        sd = jnp.dot(k_ref[lo:lo + tq, hcols], qbd, preferred_element_type=F32) + bias_ref[...]
        m = jnp.max(sd, axis=0, keepdims=True)
        so = None
        if qi > 0:
            so = jnp.dot(k_ref[0:lo, hcols], qbd, preferred_element_type=F32)
            m = jnp.maximum(m, jnp.max(so, axis=0, keepdims=True))
        return sd, so, m

    def finish(hh, qi, sd, so, m):
        lo = qi * tq
        hcols = slice(hh * LANES, (hh + 1) * LANES)
        pd = jnp.exp2(sd - m)
        l = jnp.sum(pd, axis=0, keepdims=True)
        p = pd.astype(BF16)
        if qi > 0:
            po = jnp.exp2(so - m)
            l = l + jnp.sum(po, axis=0, keepdims=True)
            p = jnp.concatenate([po.astype(BF16), p], axis=0)
        on = jnp.dot(vt_ref[hh, :, 0:lo + tq], p, preferred_element_type=F32) / l
        o = (on[:, 0:tq] - lam * on[:, tq:2 * tq]).T
        ms_o = jnp.mean(o * o, axis=-1, keepdims=True)
        o = o * lax.rsqrt(ms_o + EPS) * sub_ref[...] * (1.0 - LAM_INIT)
        o_ref[lo:lo + tq, hcols] = (o * g_ref[lo:lo + tq, hcols].astype(F32)).astype(BF16)

    order = [(hh, qi) for hh in range(ATT_HEADS) for qi in reversed(range(s_len // tq))]
    return scores, finish, order


def _mixer_kernel(rq_ref, rk_ref, rv_ref, rg_ref, gn_ref, dmat_ref, xi_ref, zeta_ref, gdec_ref,
                  q_ref, k_ref, v_ref, g_ref, bias_ref, lq1_ref, lk1_ref, lq2_ref, lk2_ref, sub_ref,
                  wo_ref, wg_ref, wp_ref,
                  ro_ref, do_ref, wo_b_ref, wg_b_ref, wp_b_ref, vt_ref):
    wo_b_ref[...] = wo_ref[...].astype(BF16)
    wg_b_ref[...] = wg_ref[...].astype(BF16)
    wp_b_ref[...] = wp_ref[...].astype(BF16)

    r_front, r_back, r_advance, r_steps = _retention_stages(
        rq_ref, rk_ref, rv_ref, rg_ref, gn_ref, dmat_ref, xi_ref, zeta_ref, gdec_ref, ro_ref)
    a_scores, a_finish, order = _attention_stages(
        q_ref, k_ref, v_ref, g_ref, bias_ref, lq1_ref, lk1_ref, lq2_ref, lk2_ref, sub_ref, do_ref, vt_ref)
    every = len(order) // r_steps

    state = jnp.zeros((2 * RET_DK, 2 * RET_DV), F32)
    r_cur = r_front(0)
    a_cur = a_scores(*order[0])
    for n, unit in enumerate(order):
        a_nxt = a_scores(*order[n + 1]) if n + 1 < len(order) else None
        if n % every == 0:
            r_n = n // every
            r_nxt = r_front(r_n + 1) if r_n + 1 < r_steps else None
            kv, scs = r_cur
            r_back(r_n, scs, state)
            state = r_advance(state, kv)
            r_cur = r_nxt
        a_finish(*unit, *a_cur)
        a_cur = a_nxt


def _out_kernel(x_ref, ro_ref, do_ref, p_ref, wo_ref, pn_ref, wg_ref, wp_ref, o_ref):
    tm = x_ref.shape[0]

    def residual(rows):
        acc = jnp.dot(ro_ref[rows, :], wo_ref[0:RET_WIDTH, :], preferred_element_type=F32)
        acc = acc + jnp.dot(do_ref[rows, :], wo_ref[RET_WIDTH:, :], preferred_element_type=F32)
        h1 = x_ref[rows, :] + acc
        ms = jnp.mean(h1 * h1, axis=-1, keepdims=True)
        return h1, (h1 * lax.rsqrt(ms + EPS) * pn_ref[...]).astype(BF16)

    def gated(rows, h1, hn):
        z = jnp.dot(hn, wg_ref[...], preferred_element_type=F32)
        gate = 1.0 / (1.0 + jnp.exp(-z))
        ple = jnp.dot(p_ref[rows, :].astype(BF16), wp_ref[...], preferred_element_type=F32)
        o_ref[rows, :] = h1 + gate * ple

    chunks = [pl.ds(r * OUT_CHUNK, OUT_CHUNK) for r in range(tm // OUT_CHUNK)]
    cur = residual(chunks[0])
    for n, rows in enumerate(chunks):
        nxt = residual(chunks[n + 1]) if n + 1 < len(chunks) else None
        gated(rows, *cur)
        cur = nxt


def _decay_tables():
    log_g = np.log1p(-np.exp2(-5.0 - np.arange(RET_HEADS, dtype=np.float64)))
    idx = np.arange(RET_T, dtype=np.float64)
    dist = np.abs(idx[:, None] - idx[None, :])
    same_or_earlier = (np.arange(RET_T)[None, :] // CHUNK) <= (np.arange(RET_T)[:, None] // CHUNK)
    dmat = np.where(same_or_earlier[None], np.exp(dist[None] * log_g[:, None, None]), 0.0)
    xi = np.exp((idx + 1.0)[None, :] * log_g[:, None])
    zeta = np.exp((RET_T - 1.0 - idx)[None, :] * log_g[:, None])
    g_step = np.exp(RET_T * log_g)
    n_pair = RET_HEADS // 2
    widen = lambda t: np.broadcast_to(t.reshape(n_pair, 2, RET_T, 1), (n_pair, 2, RET_T, RET_DV)) \
        .transpose(0, 2, 1, 3).reshape(n_pair, RET_T, 2 * RET_DV)
    xi_w = widen(xi)
    zeta_w = widen(zeta)
    rowh = (np.arange(2 * RET_DK) // RET_DK)[:, None]
    colh = (np.arange(2 * RET_DV) // RET_DV)[None, :]
    diag = (rowh == colh).astype(np.float64)
    gp = g_step.reshape(n_pair, 2)
    gq = np.where(colh[None] == 0, gp[:, 0, None, None], gp[:, 1, None, None]) * diag[None]
    gdec = np.stack([gq, np.broadcast_to(diag[None], gq.shape)], axis=1)
    return tuple(jnp.asarray(t.astype(np.float32)) for t in (dmat, xi_w, zeta_w, gdec))


def _rope_tables(s_len):
    pos = np.arange(s_len, dtype=np.float64)
    inv_freq = ROPE_BASE ** (-np.arange(RET_DK // 2, dtype=np.float64) / (RET_DK // 2))
    ang = pos[:, None] * inv_freq[None, :]
    cos, sin = np.cos(ang), np.sin(ang)
    cos_t = np.tile(cos, (1, LANES // (RET_DK // 2)))
    sin_t = np.tile(np.concatenate([-sin, sin], axis=-1), (1, LANES // RET_DK))
    return jnp.asarray(cos_t.astype(np.float32)), jnp.asarray(sin_t.astype(np.float32))


def kernel(x, p, attn_norm, w_in, ret_gn, diff_qn, diff_kn, diff_lq1, diff_lk1, diff_lq2, diff_lk2, diff_subln, w_out, ple_norm, w_ple_gate, w_ple_proj):
    b, s, d = x.shape
    m = b * s
    assert d == D_MODEL and s % PROJ_TM == 0 and m % OUT_TM == 0 and s % RET_T == 0 and s % ATT_TQ == 0
    x2 = x.reshape(m, d)
    p2 = p[0].reshape(m, PLE_DIM)

    cos_t, sin_t = _rope_tables(s)
    dmat, xi_w, zeta_w, gdec = _decay_tables()
    qn_t = jnp.tile(diff_qn[0], LANES // DIFF_DH).reshape(1, LANES)
    kn_t = jnp.tile(diff_kn[0], LANES // DIFF_DH).reshape(1, LANES)

    params = functools.partial(pltpu.CompilerParams, vmem_limit_bytes=VMEM_LIMIT_BYTES)
    const = lambda *_: (0, 0)
    pos_blocks = s // PROJ_TM

    def proj_specs(tile, col=lambda i, j: j):
        return [
            pl.BlockSpec((PROJ_TM, d), lambda i, j: (tile(i), 0)),
            pl.BlockSpec((1, d), const),
            pl.BlockSpec((d, PROJ_TN), lambda i, j: (0, col(i, j))),
            pl.BlockSpec((PROJ_TM, LANES), lambda i, j: (tile(i) % pos_blocks, 0)),
            pl.BlockSpec((PROJ_TM, LANES), lambda i, j: (tile(i) % pos_blocks, 0)),
            pl.BlockSpec((1, LANES), const),
            pl.BlockSpec((1, LANES), const),
        ]

    n_col = D_IN // PROJ_TN
    gain = attn_norm[0].reshape(1, d)
    z0, w_in_b = pl.pallas_call(
        _proj_first_kernel,
        grid=(1, n_col),
        in_specs=proj_specs(lambda i: i),
        out_specs=[pl.BlockSpec((PROJ_TM, PROJ_TN), lambda i, j: (0, j)),
                   pl.BlockSpec((d, PROJ_TN), lambda i, j: (0, j))],
        out_shape=[jax.ShapeDtypeStruct((PROJ_TM, D_IN), BF16), jax.ShapeDtypeStruct((d, D_IN), BF16)],
        scratch_shapes=[pltpu.VMEM((PROJ_TM, d), BF16)],
        compiler_params=params(dimension_semantics=("arbitrary", "arbitrary")),
        name="proj_first",
    )(x2, gain, w_in[0], cos_t, sin_t, qn_t, kn_t)
    z0_spec = pl.BlockSpec((PROJ_TM, PROJ_TN), lambda i, j: (0, jnp.where(i == 0, j, n_col - 1)))
    z = pl.pallas_call(
        _proj_rest_kernel,
        grid=(m // PROJ_TM, n_col),
        in_specs=proj_specs(lambda i: jnp.maximum(i, 1), lambda i, j: jnp.where(i == 0, 0, j)) + [z0_spec],
        out_specs=pl.BlockSpec((PROJ_TM, PROJ_TN), lambda i, j: (i, j)),
        out_shape=jax.ShapeDtypeStruct((m, D_IN), BF16),
        scratch_shapes=[pltpu.VMEM((PROJ_TM, d), BF16)],
        compiler_params=params(dimension_semantics=("arbitrary", "arbitrary")),
        name="proj_rest",
    )(x2, gain, w_in_b, cos_t, sin_t, qn_t, kn_t, z0)

    z3 = z.reshape(b, s, D_IN)
    vec = lambda a: a[0].reshape(1, -1)
    tile_chunk = np.arange(ATT_TK) // CHUNK
    diag_bias = np.where(tile_chunk[:, None] <= tile_chunk[None, :], 0.0, NEG_BIG).astype(np.float32)
    diag_bias = jnp.asarray(np.tile(diag_bias, (1, 2)))
    hw = ATT_HEADS * LANES
    assert (DQ_BLK % ATT_HEADS, DK_BLK % ATT_HEADS, DV_BLK % ATT_HEADS, DG_BLK % ATT_HEADS) == (0, 0, 0, 0)
    n_att = DIFF_HEADS // ATT_HEADS
    assert n_att == RET_HEADS // 2
    w_slice = lambda rows: pl.BlockSpec((rows // (b * n_att), d), lambda bi, h: (bi * n_att + h, 0))
    ro, do, wo_b, wg_b, wp_b = pl.pallas_call(
        _mixer_kernel,
        grid=(b, n_att),
        in_specs=[
            pl.BlockSpec((None, s, LANES), lambda bi, hp: (bi, 0, RQ_BLK + hp)),
            pl.BlockSpec((None, s, LANES), lambda bi, hp: (bi, 0, RK_BLK + hp)),
            pl.BlockSpec((None, s, 2 * RET_DV), lambda bi, hp: (bi, 0, 1024 // 256 + hp)),
            pl.BlockSpec((None, s, 2 * RET_DV), lambda bi, hp: (bi, 0, 2048 // 256 + hp)),
            pl.BlockSpec((1, 2 * RET_DV), lambda bi, hp: (0, hp)),
            pl.BlockSpec((2, RET_T, RET_T), lambda bi, hp: (hp, 0, 0)),
            pl.BlockSpec((None, RET_T, 2 * RET_DV), lambda bi, hp: (hp, 0, 0)),
            pl.BlockSpec((None, RET_T, 2 * RET_DV), lambda bi, hp: (hp, 0, 0)),
            pl.BlockSpec((None, 2, 2 * RET_DK, 2 * RET_DV), lambda bi, hp: (hp, 0, 0, 0)),
            pl.BlockSpec((None, s, hw), lambda bi, h: (bi, 0, DQ_BLK // ATT_HEADS + h)),
            pl.BlockSpec((None, s, hw), lambda bi, h: (bi, 0, DK_BLK // ATT_HEADS + h)),
            pl.BlockSpec((None, s, hw), lambda bi, h: (bi, 0, DV_BLK // ATT_HEADS + h)),
            pl.BlockSpec((None, s, hw), lambda bi, h: (bi, 0, DG_BLK // ATT_HEADS + h)),
            pl.BlockSpec((ATT_TK, 2 * ATT_TQ), lambda *_: (0, 0)),
            pl.BlockSpec((1, DIFF_DH), lambda *_: (0, 0)),
            pl.BlockSpec((1, DIFF_DH), lambda *_: (0, 0)),
            pl.BlockSpec((1, DIFF_DH), lambda *_: (0, 0)),
            pl.BlockSpec((1, DIFF_DH), lambda *_: (0, 0)),
            pl.BlockSpec((1, DIFF_DV), lambda *_: (0, 0)),
            w_slice(d), w_slice(d), w_slice(PLE_DIM),
        ],
        out_specs=[pl.BlockSpec((None, s, 2 * RET_DV), lambda bi, hp: (bi, 0, hp)),
                   pl.BlockSpec((None, s, hw), lambda bi, h: (bi, 0, h)),
                   w_slice(d), w_slice(d), w_slice(PLE_DIM)],
        out_shape=[jax.ShapeDtypeStruct((b, s, RET_WIDTH), BF16),
                   jax.ShapeDtypeStruct((b, s, DIFF_WIDTH), BF16),
                   jax.ShapeDtypeStruct((d, d), BF16), jax.ShapeDtypeStruct((d, d), BF16),
                   jax.ShapeDtypeStruct((PLE_DIM, d), BF16)],
        scratch_shapes=[pltpu.VMEM((ATT_HEADS, DIFF_DV, s), BF16)],
        compiler_params=params(dimension_semantics=("arbitrary", "arbitrary")),
        name="mixer",
    )(z3, z3, z3, z3, ret_gn[0].reshape(1, RET_WIDTH), dmat, xi_w, zeta_w, gdec,
      z3, z3, z3, z3, diag_bias, vec(diff_lq1), vec(diff_lk1), vec(diff_lq2), vec(diff_lk2), vec(diff_subln),
      w_out[0], w_ple_gate[0], w_ple_proj[0])

    resident = functools.partial(pl.BlockSpec, pipeline_mode=pl.Buffered(1))
    out = pl.pallas_call(
        _out_kernel,
        grid=(m // OUT_TM,),
        in_specs=[
            pl.BlockSpec((OUT_TM, d), lambda i: (i, 0)),
            pl.BlockSpec((OUT_TM, RET_WIDTH), lambda i: (i, 0)),
            pl.BlockSpec((OUT_TM, DIFF_WIDTH), lambda i: (i, 0)),
            pl.BlockSpec((OUT_TM, PLE_DIM), lambda i: (i, 0)),
            resident((d, d), lambda i: (0, 0)),
            pl.BlockSpec((1, d), lambda i: (0, 0)),
            resident((d, d), lambda i: (0, 0)),
            resident((PLE_DIM, d), lambda i: (0, 0)),
        ],
        out_specs=pl.BlockSpec((OUT_TM, d), lambda i: (i, 0)),
        out_shape=jax.ShapeDtypeStruct((m, d), F32),
        compiler_params=params(dimension_semantics=("arbitrary",)),
        name="outproj",
    )(x2, ro.reshape(m, RET_WIDTH), do.reshape(m, DIFF_WIDTH), p2, wo_b, ple_norm[0].reshape(1, d), wg_b, wp_b)

    return out.reshape(b, s, d)
```

```python
import functools
import math

import jax
import jax.numpy as jnp
import numpy as np
from jax import lax
from jax.experimental import pallas as pl
from jax.experimental.pallas import tpu as pltpu

F32 = jnp.float32
BF16 = jnp.bfloat16

D_MODEL = 2048
CHUNK = 64
PLE_DIM = 256
RET_WIDTH = 1024
DIFF_WIDTH = 1024
RET_HEADS = 8
RET_DV = 128
RET_DK = 64
DIFF_HEADS = 8
DIFF_DV = 128
DIFF_DH = 64
ROPE_BASE = 10000.0
EPS = 1e-6
D_IN = 7168
LAM_INIT = 0.8 - 0.6 * math.exp(-0.3 * 0)

LANES = 128
VMEM_LIMIT_BYTES = 56 * 1024 * 1024

RQ_BLK = 0
RK_BLK = 512 // LANES
RV_COL = 1024
RG_COL = 2048
DQ_BLK = 3072 // LANES
DK_BLK = 4096 // LANES
DV_BLK = 5120 // LANES
DG_BLK = 6144 // LANES

PROJ_TM = 1024
PROJ_TN = 1024
ROW_CHUNK = 256
RET_T = 256
ATT_TQ = 256
ATT_TK = 256
ATT_HEADS = 2
OUT_TM = 512
OUT_CHUNK = 256
LOG2E = math.log2(math.e)
NEG_BIG = -1e30


def _silu(a):
    return a * (1.0 / (1.0 + jnp.exp(-a)))


def _swap_halves(a):
    lane = lax.broadcasted_iota(jnp.int32, a.shape, 1)
    first = (lane % RET_DK) < (RET_DK // 2)
    return jnp.where(first, pltpu.roll(a, LANES - RET_DK // 2, 1), pltpu.roll(a, RET_DK // 2, 1))


def _proj_kernel(x_ref, g_ref, w_ref, cos_ref, sin_ref, qn_ref, kn_ref, o_ref, u_ref, wb_ref):
    j = pl.program_id(1)
    tm = x_ref.shape[0]
    tn = w_ref.shape[1]

    def rope_epilogue(a, rows):
        c = cos_ref[rows, :]
        s = sin_ref[rows, :]
        for sl in range(tn // LANES):
            scale = 1.0 if sl * LANES < RET_HEADS * RET_DK else RET_DK ** -0.5
            asl = a[:, sl * LANES:(sl + 1) * LANES]
            o_ref[rows, pl.ds(sl * LANES, LANES)] = ((asl * c + _swap_halves(asl) * s) * scale).astype(BF16)

    def norm_epilogue(gn_ref, scale):
        def apply(a, rows):
            low = lax.broadcasted_iota(jnp.int32, (1, LANES), 1) < DIFF_DH
            for sl in range(tn // LANES):
                asl = a[:, sl * LANES:(sl + 1) * LANES]
                sq = asl * asl
                s_lo = jnp.sum(jnp.where(low, sq, 0.0), axis=-1, keepdims=True)
                s_hi = jnp.sum(jnp.where(low, 0.0, sq), axis=-1, keepdims=True)
                ms = jnp.where(low, s_lo, s_hi) * (1.0 / DIFF_DH)
                o_ref[rows, pl.ds(sl * LANES, LANES)] = (asl * lax.rsqrt(ms + EPS) * gn_ref[...] * scale).astype(BF16)
        return apply

    def silu_epilogue(a, rows):
        o_ref[rows, :] = _silu(a).astype(BF16)

    def plain_epilogue(a, rows):
        o_ref[rows, :] = a.astype(BF16)

    def run(epilogue, normalize_input=False):
        sizes = [ROW_CHUNK] * (tm // ROW_CHUNK - 1) + [ROW_CHUNK // 2] * 2
        for kt in range(w_ref.shape[0] // ROW_CHUNK):
            krows = pl.ds(kt * ROW_CHUNK, ROW_CHUNK)
            wb_ref[krows, :] = w_ref[krows, :].astype(BF16)
        for r, size in enumerate(sizes):
            rows = pl.ds(sum(sizes[:r]), size)
            if normalize_input:
                x = x_ref[rows, :]
                ms = jnp.mean(x * x, axis=-1, keepdims=True)
                u_ref[rows, :] = (x * lax.rsqrt(ms + EPS) * g_ref[...]).astype(BF16)
            epilogue(jnp.dot(u_ref[rows, :], wb_ref[...], preferred_element_type=F32), rows)

    @pl.when(j == 0)
    def _():
        run(rope_epilogue, normalize_input=True)

    @pl.when((j == 1) | (j == 5))
    def _():
        run(plain_epilogue)

    @pl.when((j == 2) | (j == 6))
    def _():
        run(silu_epilogue)

    @pl.when(j == 3)
    def _():
        run(norm_epilogue(qn_ref, DIFF_DH ** -0.5 * LOG2E))

    @pl.when(j == 4)
    def _():
        run(norm_epilogue(kn_ref, 1.0))


def _retention_stages(q_ref, k_ref, v_ref, g_ref, gn_ref, dmat_ref, xi_ref, zeta_ref, gdec_ref, o_ref):
    s_len = q_ref.shape[0]
    lane = lax.broadcasted_iota(jnp.int32, (1, LANES), 1)

    def front(n):
        rows = pl.ds(n * RET_T, RET_T)
        q = q_ref[rows, :]
        k = k_ref[rows, :]
        vz = (v_ref[rows, :].astype(F32) * zeta_ref[...]).astype(BF16)
        kv = lax.dot_general(k, vz, (((0,), (0,)), ((), ())), preferred_element_type=F32)
        scs = []
        for hh in range(2):
            qm = jnp.where((lane < RET_DK) == (hh == 0), q, jnp.zeros_like(q))
            sc = lax.dot_general(qm, k, (((1,), (1,)), ((), ())), preferred_element_type=F32)
            scs.append((sc * dmat_ref[hh]).astype(BF16))
        return kv, scs

    def back(n, scs, state):
        rows = pl.ds(n * RET_T, RET_T)
        cross = jnp.dot(q_ref[rows, :], state.astype(BF16), preferred_element_type=F32)
        for hh in range(2):
            cols = pl.ds(hh * RET_DV, RET_DV)
            intra = jnp.dot(scs[hh], v_ref[rows, cols], preferred_element_type=F32)
            o = intra + cross[:, hh * RET_DV:(hh + 1) * RET_DV] * xi_ref[:, cols]
            ms = jnp.mean(o * o, axis=-1, keepdims=True)
            o = o * lax.rsqrt(ms + EPS) * gn_ref[:, cols]
            o_ref[rows, cols] = (o * g_ref[rows, cols].astype(F32)).astype(BF16)

    def advance(state, kv):
        return state * gdec_ref[0] + kv * gdec_ref[1]

    return front, back, advance, s_len // RET_T


def _attention_stages(q_ref, k_ref, v_ref, g_ref, bias_ref, lq1_ref, lk1_ref, lq2_ref, lk2_ref, sub_ref, o_ref,
                      vt_ref):
    s_len = k_ref.shape[0]
    tq = ATT_TQ
    for hh in range(ATT_HEADS):
        for t in range(s_len // ATT_TK):
            cols = pl.ds(t * ATT_TK, ATT_TK)
            vt_ref[hh, :, cols] = v_ref[cols, hh * DIFF_DV:(hh + 1) * DIFF_DV].astype(F32).T.astype(BF16)

    lam = (jnp.exp(jnp.sum(lq1_ref[...] * lk1_ref[...], axis=-1, keepdims=True))
           - jnp.exp(jnp.sum(lq2_ref[...] * lk2_ref[...], axis=-1, keepdims=True)) + LAM_INIT)

    def scores(hh, qi):
        lo = qi * tq
        hcols = slice(hh * LANES, (hh + 1) * LANES)
        qt = q_ref[lo:lo + tq, hcols].astype(F32).T
        row = lax.broadcasted_iota(jnp.int32, qt.shape, 0)
        qbd = jnp.concatenate([jnp.where(row < DIFF_DH, qt, 0.0), jnp.where(row >= DIFF_DH, qt, 0.0)],
                              axis=1).astype(BF16)
        sd = jnp.dot(k_ref[lo:lo + tq, hcols], qbd, preferred_element_type=F32) + bias_ref[...]
        m = jnp.max(sd, axis=0, keepdims=True)
        so = None
        if qi > 0:
            so = jnp.dot(k_ref[0:lo, hcols], qbd, preferred_element_type=F32)
            m = jnp.maximum(m, jnp.max(so, axis=0, keepdims=True))
        return sd, so, m

    def finish(hh, qi, sd, so, m):
        lo = qi * tq
        hcols = slice(hh * LANES, (hh + 1) * LANES)
        pd = jnp.exp2(sd - m)
        l = jnp.sum(pd, axis=0, keepdims=True)
        p = pd.astype(BF16)
        if qi > 0:
            po = jnp.exp2(so - m)
            l = l + jnp.sum(po, axis=0, keepdims=True)
            p = jnp.concatenate([po.astype(BF16), p], axis=0)
        on = jnp.dot(vt_ref[hh, :, 0:lo + tq], p, preferred_element_type=F32) / l
        o = (on[:, 0:tq] - lam * on[:, tq:2 * tq]).T
        ms_o = jnp.mean(o * o, axis=-1, keepdims=True)
        o = o * lax.rsqrt(ms_o + EPS) * sub_ref[...] * (1.0 - LAM_INIT)
        o_ref[lo:lo + tq, hcols] = (o * g_ref[lo:lo + tq, hcols].astype(F32)).astype(BF16)

    order = [(hh, qi) for hh in range(ATT_HEADS) for qi in reversed(range(s_len // tq))]
    return scores, finish, order


def _mixer_kernel(rq_ref, rk_ref, rv_ref, rg_ref, gn_ref, dmat_ref, xi_ref, zeta_ref, gdec_ref,
                  q_ref, k_ref, v_ref, g_ref, bias_ref, lq1_ref, lk1_ref, lq2_ref, lk2_ref, sub_ref,
                  wo_ref, wg_ref, wp_ref,
                  ro_ref, do_ref, wo_b_ref, wg_b_ref, wp_b_ref, vt_ref):
    wo_b_ref[...] = wo_ref[...].astype(BF16)
    wg_b_ref[...] = wg_ref[...].astype(BF16)
    wp_b_ref[...] = wp_ref[...].astype(BF16)

    r_front, r_back, r_advance, r_steps = _retention_stages(
        rq_ref, rk_ref, rv_ref, rg_ref, gn_ref, dmat_ref, xi_ref, zeta_ref, gdec_ref, ro_ref)
    a_scores, a_finish, order = _attention_stages(
        q_ref, k_ref, v_ref, g_ref, bias_ref, lq1_ref, lk1_ref, lq2_ref, lk2_ref, sub_ref, do_ref, vt_ref)
    every = len(order) // r_steps

    state = jnp.zeros((2 * RET_DK, 2 * RET_DV), F32)
    r_cur = r_front(0)
    a_cur = a_scores(*order[0])
    for n, unit in enumerate(order):
        a_nxt = a_scores(*order[n + 1]) if n + 1 < len(order) else None
        if n % every == 0:
            r_n = n // every
            r_nxt = r_front(r_n + 1) if r_n + 1 < r_steps else None
            kv, scs = r_cur
            r_back(r_n, scs, state)
            state = r_advance(state, kv)
            r_cur = r_nxt
        a_finish(*unit, *a_cur)
        a_cur = a_nxt


def _out_kernel(x_ref, ro_ref, do_ref, p_ref, wo_ref, pn_ref, wg_ref, wp_ref, o_ref):
    tm = x_ref.shape[0]

    def residual(rows):
        acc = jnp.dot(ro_ref[rows, :], wo_ref[0:RET_WIDTH, :], preferred_element_type=F32)
        acc = acc + jnp.dot(do_ref[rows, :], wo_ref[RET_WIDTH:, :], preferred_element_type=F32)
        h1 = x_ref[rows, :] + acc
        ms = jnp.mean(h1 * h1, axis=-1, keepdims=True)
        return h1, (h1 * lax.rsqrt(ms + EPS) * pn_ref[...]).astype(BF16)

    def gated(rows, h1, hn):
        z = jnp.dot(hn, wg_ref[...], preferred_element_type=F32)
        gate = 1.0 / (1.0 + jnp.exp(-z))
        ple = jnp.dot(p_ref[rows, :].astype(BF16), wp_ref[...], preferred_element_type=F32)
        o_ref[rows, :] = h1 + gate * ple

    chunks = [pl.ds(r * OUT_CHUNK, OUT_CHUNK) for r in range(tm // OUT_CHUNK)]
    cur = residual(chunks[0])
    for n, rows in enumerate(chunks):
        nxt = residual(chunks[n + 1]) if n + 1 < len(chunks) else None
        gated(rows, *cur)
        cur = nxt


def _decay_tables():
    log_g = np.log1p(-np.exp2(-5.0 - np.arange(RET_HEADS, dtype=np.float64)))
    idx = np.arange(RET_T, dtype=np.float64)
    dist = np.abs(idx[:, None] - idx[None, :])
    same_or_earlier = (np.arange(RET_T)[None, :] // CHUNK) <= (np.arange(RET_T)[:, None] // CHUNK)
    dmat = np.where(same_or_earlier[None], np.exp(dist[None] * log_g[:, None, None]), 0.0)
    xi = np.exp((idx + 1.0)[None, :] * log_g[:, None])
    zeta = np.exp((RET_T - 1.0 - idx)[None, :] * log_g[:, None])
    g_step = np.exp(RET_T * log_g)
    n_pair = RET_HEADS // 2
    widen = lambda t: np.broadcast_to(t.reshape(n_pair, 2, RET_T, 1), (n_pair, 2, RET_T, RET_DV)) \
        .transpose(0, 2, 1, 3).reshape(n_pair, RET_T, 2 * RET_DV)
    xi_w = widen(xi)
    zeta_w = widen(zeta)
    rowh = (np.arange(2 * RET_DK) // RET_DK)[:, None]
    colh = (np.arange(2 * RET_DV) // RET_DV)[None, :]
    diag = (rowh == colh).astype(np.float64)
    gp = g_step.reshape(n_pair, 2)
    gq = np.where(colh[None] == 0, gp[:, 0, None, None], gp[:, 1, None, None]) * diag[None]
    gdec = np.stack([gq, np.broadcast_to(diag[None], gq.shape)], axis=1)
    return tuple(jnp.asarray(t.astype(np.float32)) for t in (dmat, xi_w, zeta_w, gdec))


def _rope_tables(s_len):
    pos = np.arange(s_len, dtype=np.float64)
    inv_freq = ROPE_BASE ** (-np.arange(RET_DK // 2, dtype=np.float64) / (RET_DK // 2))
    ang = pos[:, None] * inv_freq[None, :]
    cos, sin = np.cos(ang), np.sin(ang)
    cos_t = np.tile(cos, (1, LANES // (RET_DK // 2)))
    sin_t = np.tile(np.concatenate([-sin, sin], axis=-1), (1, LANES // RET_DK))
    return jnp.asarray(cos_t.astype(np.float32)), jnp.asarray(sin_t.astype(np.float32))


def kernel(x, p, attn_norm, w_in, ret_gn, diff_qn, diff_kn, diff_lq1, diff_lk1, diff_lq2, diff_lk2, diff_subln, w_out, ple_norm, w_ple_gate, w_ple_proj):
    b, s, d = x.shape
    m = b * s
    assert d == D_MODEL and s % PROJ_TM == 0 and m % OUT_TM == 0 and s % RET_T == 0 and s % ATT_TQ == 0
    x2 = x.reshape(m, d)
    p2 = p[0].reshape(m, PLE_DIM)

    cos_t, sin_t = _rope_tables(s)
    dmat, xi_w, zeta_w, gdec = _decay_tables()
    qn_t = jnp.tile(diff_qn[0], LANES // DIFF_DH).reshape(1, LANES)
    kn_t = jnp.tile(diff_kn[0], LANES // DIFF_DH).reshape(1, LANES)

    params = functools.partial(pltpu.CompilerParams, vmem_limit_bytes=VMEM_LIMIT_BYTES)
    const = lambda *_: (0, 0)
    pos_blocks = s // PROJ_TM

    n_seg = D_IN // PROJ_TN
    z = pl.pallas_call(
        _proj_kernel,
        grid=(m // PROJ_TM, n_seg),
        in_specs=[
            pl.BlockSpec((PROJ_TM, d), lambda i, j: (i, 0)),
            pl.BlockSpec((1, d), const),
            pl.BlockSpec((d, PROJ_TN), lambda i, j: (0, j)),
            pl.BlockSpec((PROJ_TM, LANES), lambda i, j: (i % pos_blocks, 0)),
            pl.BlockSpec((PROJ_TM, LANES), lambda i, j: (i % pos_blocks, 0)),
            pl.BlockSpec((1, LANES), const),
            pl.BlockSpec((1, LANES), const),
        ],
        out_specs=pl.BlockSpec((None, PROJ_TM, PROJ_TN), lambda i, j: (j, i, 0)),
        out_shape=jax.ShapeDtypeStruct((n_seg, m, PROJ_TN), BF16),
        scratch_shapes=[pltpu.VMEM((PROJ_TM, d), BF16), pltpu.VMEM((d, PROJ_TN), BF16)],
        compiler_params=params(dimension_semantics=("arbitrary", "arbitrary")),
        name="proj",
    )(x2, attn_norm[0].reshape(1, d), w_in[0], cos_t, sin_t, qn_t, kn_t)

    z4 = z.reshape(n_seg, b, s, PROJ_TN)

    def z_block(col0, width):
        seg, blk0 = col0 // PROJ_TN, (col0 % PROJ_TN) // width
        return pl.BlockSpec((None, None, s, width), lambda bi, h: (seg, bi, 0, blk0 + h))
    vec = lambda a: a[0].reshape(1, -1)
    tile_chunk = np.arange(ATT_TK) // CHUNK
    diag_bias = np.where(tile_chunk[:, None] <= tile_chunk[None, :], 0.0, NEG_BIG).astype(np.float32)
    diag_bias = jnp.asarray(np.tile(diag_bias, (1, 2)))
    hw = ATT_HEADS * LANES
    assert (DQ_BLK % ATT_HEADS, DK_BLK % ATT_HEADS, DV_BLK % ATT_HEADS, DG_BLK % ATT_HEADS) == (0, 0, 0, 0)
    n_att = DIFF_HEADS // ATT_HEADS
    assert n_att == RET_HEADS // 2
    w_slice = lambda rows: pl.BlockSpec((rows // (b * n_att), d), lambda bi, h: (bi * n_att + h, 0))
    ro, do, wo_b, wg_b, wp_b = pl.pallas_call(
        _mixer_kernel,
        grid=(b, n_att),
        in_specs=[
            z_block(RQ_BLK * LANES, LANES),
            z_block(RK_BLK * LANES, LANES),
            z_block(RV_COL, 2 * RET_DV),
            z_block(RG_COL, 2 * RET_DV),
            pl.BlockSpec((1, 2 * RET_DV), lambda bi, hp: (0, hp)),
            pl.BlockSpec((2, RET_T, RET_T), lambda bi, hp: (hp, 0, 0)),
            pl.BlockSpec((None, RET_T, 2 * RET_DV), lambda bi, hp: (hp, 0, 0)),
            pl.BlockSpec((None, RET_T, 2 * RET_DV), lambda bi, hp: (hp, 0, 0)),
            pl.BlockSpec((None, 2, 2 * RET_DK, 2 * RET_DV), lambda bi, hp: (hp, 0, 0, 0)),
            z_block(DQ_BLK * LANES, hw),
            z_block(DK_BLK * LANES, hw),
            z_block(DV_BLK * LANES, hw),
            z_block(DG_BLK * LANES, hw),
            pl.BlockSpec((ATT_TK, 2 * ATT_TQ), lambda *_: (0, 0)),
            pl.BlockSpec((1, DIFF_DH), lambda *_: (0, 0)),
            pl.BlockSpec((1, DIFF_DH), lambda *_: (0, 0)),
            pl.BlockSpec((1, DIFF_DH), lambda *_: (0, 0)),
            pl.BlockSpec((1, DIFF_DH), lambda *_: (0, 0)),
            pl.BlockSpec((1, DIFF_DV), lambda *_: (0, 0)),
            w_slice(d), w_slice(d), w_slice(PLE_DIM),
        ],
        out_specs=[pl.BlockSpec((None, s, 2 * RET_DV), lambda bi, hp: (bi, 0, hp)),
                   pl.BlockSpec((None, s, hw), lambda bi, h: (bi, 0, h)),
                   w_slice(d), w_slice(d), w_slice(PLE_DIM)],
        out_shape=[jax.ShapeDtypeStruct((b, s, RET_WIDTH), BF16),
                   jax.ShapeDtypeStruct((b, s, DIFF_WIDTH), BF16),
                   jax.ShapeDtypeStruct((d, d), BF16), jax.ShapeDtypeStruct((d, d), BF16),
                   jax.ShapeDtypeStruct((PLE_DIM, d), BF16)],
        scratch_shapes=[pltpu.VMEM((ATT_HEADS, DIFF_DV, s), BF16)],
        compiler_params=params(dimension_semantics=("arbitrary", "arbitrary")),
        name="mixer",
    )(z4, z4, z4, z4, ret_gn[0].reshape(1, RET_WIDTH), dmat, xi_w, zeta_w, gdec,
      z4, z4, z4, z4, diag_bias, vec(diff_lq1), vec(diff_lk1), vec(diff_lq2), vec(diff_lk2), vec(diff_subln),
      w_out[0], w_ple_gate[0], w_ple_proj[0])

    resident = functools.partial(pl.BlockSpec, pipeline_mode=pl.Buffered(1))
    out = pl.pallas_call(
        _out_kernel,
        grid=(m // OUT_TM,),
        in_specs=[
            pl.BlockSpec((OUT_TM, d), lambda i: (i, 0)),
            pl.BlockSpec((OUT_TM, RET_WIDTH), lambda i: (i, 0)),
            pl.BlockSpec((OUT_TM, DIFF_WIDTH), lambda i: (i, 0)),
            pl.BlockSpec((OUT_TM, PLE_DIM), lambda i: (i, 0)),
            resident((d, d), lambda i: (0, 0)),
            pl.BlockSpec((1, d), lambda i: (0, 0)),
            resident((d, d), lambda i: (0, 0)),
            resident((PLE_DIM, d), lambda i: (0, 0)),
        ],
        out_specs=pl.BlockSpec((OUT_TM, d), lambda i: (i, 0)),
        out_shape=jax.ShapeDtypeStruct((m, d), F32),
        compiler_params=params(dimension_semantics=("arbitrary",)),
        name="outproj",
    )(x2, ro.reshape(m, RET_WIDTH), do.reshape(m, DIFF_WIDTH), p2, wo_b, ple_norm[0].reshape(1, d), wg_b, wp_b)

    return out.reshape(b, s, d)
```

```python
import functools
import math

import jax
import jax.numpy as jnp
import numpy as np
from jax import lax
from jax.experimental import pallas as pl
from jax.experimental.pallas import tpu as pltpu

F32 = jnp.float32
BF16 = jnp.bfloat16

D_MODEL = 2048
CHUNK = 64
PLE_DIM = 256
RET_WIDTH = 1024
DIFF_WIDTH = 1024
RET_HEADS = 8
RET_DV = 128
RET_DK = 64
DIFF_HEADS = 8
DIFF_DV = 128
DIFF_DH = 64
ROPE_BASE = 10000.0
EPS = 1e-6
D_IN = 7168
LAM_INIT = 0.8 - 0.6 * math.exp(-0.3 * 0)

LANES = 128
VMEM_LIMIT_BYTES = 56 * 1024 * 1024

RQ_BLK = 0
RK_BLK = 512 // LANES
RV_COL = 1024
RG_COL = 2048
DQ_BLK = 3072 // LANES
DK_BLK = 4096 // LANES
DV_BLK = 5120 // LANES
DG_BLK = 6144 // LANES

PROJ_TM = 1024
PROJ_TN = 1024
ROW_CHUNK = 256
RET_T = 256
ATT_TQ = 256
ATT_TK = 256
ATT_HEADS = 2
OUT_TM = 512
OUT_CHUNK = 256
LOG2E = math.log2(math.e)
NEG_BIG = -1e30


def _silu(a):
    return a * (1.0 / (1.0 + jnp.exp(-a)))


def _swap_halves(a):
    lane = lax.broadcasted_iota(jnp.int32, a.shape, 1)
    first = (lane % RET_DK) < (RET_DK // 2)
    return jnp.where(first, pltpu.roll(a, LANES - RET_DK // 2, 1), pltpu.roll(a, RET_DK // 2, 1))


def _proj_kernel(xa_ref, xb_ref, g_ref, w_ref, cos_ref, sin_ref, qn_ref, kn_ref, o_ref, u_ref, wb_ref):
    j = pl.program_id(1)
    tm = xa_ref.shape[0]
    half = xa_ref.shape[1]
    tn = w_ref.shape[1]

    def rope_epilogue(a, rows):
        c = cos_ref[rows, :]
        s = sin_ref[rows, :]
        for sl in range(tn // LANES):
            scale = 1.0 if sl * LANES < RET_HEADS * RET_DK else RET_DK ** -0.5
            asl = a[:, sl * LANES:(sl + 1) * LANES]
            o_ref[rows, pl.ds(sl * LANES, LANES)] = ((asl * c + _swap_halves(asl) * s) * scale).astype(BF16)

    def norm_epilogue(gn_ref, scale):
        def apply(a, rows):
            low = lax.broadcasted_iota(jnp.int32, (1, LANES), 1) < DIFF_DH
            for sl in range(tn // LANES):
                asl = a[:, sl * LANES:(sl + 1) * LANES]
                sq = asl * asl
                s_lo = jnp.sum(jnp.where(low, sq, 0.0), axis=-1, keepdims=True)
                s_hi = jnp.sum(jnp.where(low, 0.0, sq), axis=-1, keepdims=True)
                ms = jnp.where(low, s_lo, s_hi) * (1.0 / DIFF_DH)
                o_ref[rows, pl.ds(sl * LANES, LANES)] = (asl * lax.rsqrt(ms + EPS) * gn_ref[...] * scale).astype(BF16)
        return apply

    def silu_epilogue(a, rows):
        o_ref[rows, :] = _silu(a).astype(BF16)

    def plain_epilogue(a, rows):
        o_ref[rows, :] = a.astype(BF16)

    def run(epilogue, normalize_input=False):
        sizes = [ROW_CHUNK] * (tm // ROW_CHUNK - 1) + [ROW_CHUNK // 2] * 2
        for kt in range(w_ref.shape[0] // ROW_CHUNK):
            krows = pl.ds(kt * ROW_CHUNK, ROW_CHUNK)
            wb_ref[krows, :] = w_ref[krows, :].astype(BF16)
        for r, size in enumerate(sizes):
            rows = pl.ds(sum(sizes[:r]), size)
            if normalize_input:
                xa = xa_ref[rows, :]
                xb = xb_ref[rows, :]
                ss = jnp.sum(xa * xa, axis=-1, keepdims=True) + jnp.sum(xb * xb, axis=-1, keepdims=True)
                inv = lax.rsqrt(ss * (1.0 / (2 * half)) + EPS)
                u_ref[rows, 0:half] = (xa * inv * g_ref[:, 0:half]).astype(BF16)
                u_ref[rows, half:2 * half] = (xb * inv * g_ref[:, half:2 * half]).astype(BF16)
            epilogue(jnp.dot(u_ref[rows, :], wb_ref[...], preferred_element_type=F32), rows)

    @pl.when(j == 0)
    def _():
        run(rope_epilogue, normalize_input=True)

    @pl.when((j == 1) | (j == 5))
    def _():
        run(plain_epilogue)

    @pl.when((j == 2) | (j == 6))
    def _():
        run(silu_epilogue)

    @pl.when(j == 3)
    def _():
        run(norm_epilogue(qn_ref, DIFF_DH ** -0.5 * LOG2E))

    @pl.when(j == 4)
    def _():
        run(norm_epilogue(kn_ref, 1.0))


def _retention_stages(q_ref, k_ref, v_ref, g_ref, gn_ref, dmat_ref, xi_ref, zeta_ref, gdec_ref, o_ref):
    s_len = q_ref.shape[0]
    lane = lax.broadcasted_iota(jnp.int32, (1, LANES), 1)

    def front(n):
        rows = pl.ds(n * RET_T, RET_T)
        q = q_ref[rows, :]
        k = k_ref[rows, :]
        vz = (v_ref[rows, :].astype(F32) * zeta_ref[...]).astype(BF16)
        kv = lax.dot_general(k, vz, (((0,), (0,)), ((), ())), preferred_element_type=F32)
        scs = []
        for hh in range(2):
            qm = jnp.where((lane < RET_DK) == (hh == 0), q, jnp.zeros_like(q))
            sc = lax.dot_general(qm, k, (((1,), (1,)), ((), ())), preferred_element_type=F32)
            scs.append((sc * dmat_ref[hh]).astype(BF16))
        return kv, scs

    def back(n, scs, state):
        rows = pl.ds(n * RET_T, RET_T)
        cross = jnp.dot(q_ref[rows, :], state.astype(BF16), preferred_element_type=F32)
        for hh in range(2):
            cols = pl.ds(hh * RET_DV, RET_DV)
            intra = jnp.dot(scs[hh], v_ref[rows, cols], preferred_element_type=F32)
            o = intra + cross[:, hh * RET_DV:(hh + 1) * RET_DV] * xi_ref[:, cols]
            ms = jnp.mean(o * o, axis=-1, keepdims=True)
            o = o * lax.rsqrt(ms + EPS) * gn_ref[:, cols]
            o_ref[rows, cols] = (o * g_ref[rows, cols].astype(F32)).astype(BF16)

    def advance(state, kv):
        return state * gdec_ref[0] + kv * gdec_ref[1]

    return front, back, advance, s_len // RET_T


def _attention_stages(q_ref, k_ref, v_ref, g_ref, bias_ref, lq1_ref, lk1_ref, lq2_ref, lk2_ref, sub_ref, o_ref,
                      vt_ref):
    s_len = k_ref.shape[0]
    tq = ATT_TQ
    for hh in range(ATT_HEADS):
        for t in range(s_len // ATT_TK):
            cols = pl.ds(t * ATT_TK, ATT_TK)
            vt_ref[hh, :, cols] = v_ref[cols, hh * DIFF_DV:(hh + 1) * DIFF_DV].astype(F32).T.astype(BF16)

    lam = (jnp.exp(jnp.sum(lq1_ref[...] * lk1_ref[...], axis=-1, keepdims=True))
           - jnp.exp(jnp.sum(lq2_ref[...] * lk2_ref[...], axis=-1, keepdims=True)) + LAM_INIT)

    def scores(hh, qi):
        lo = qi * tq
        hcols = slice(hh * LANES, (hh + 1) * LANES)
        qt = q_ref[lo:lo + tq, hcols].astype(F32).T
        row = lax.broadcasted_iota(jnp.int32, qt.shape, 0)
        qbd = jnp.concatenate([jnp.where(row < DIFF_DH, qt, 0.0), jnp.where(row >= DIFF_DH, qt, 0.0)],
                              axis=1).astype(BF16)
        sd = jnp.dot(k_ref[lo:lo + tq, hcols], qbd, preferred_element_type=F32) + bias_ref[...]
        m = jnp.max(sd, axis=0, keepdims=True)
        so = None
        if qi > 0:
            so = jnp.dot(k_ref[0:lo, hcols], qbd, preferred_element_type=F32)
            m = jnp.maximum(m, jnp.max(so, axis=0, keepdims=True))
        return sd, so, m

    def finish(hh, qi, sd, so, m):
        lo = qi * tq
        hcols = slice(hh * LANES, (hh + 1) * LANES)
        pd = jnp.exp2(sd - m)
        l = jnp.sum(pd, axis=0, keepdims=True)
        p = pd.astype(BF16)
        if qi > 0:
            po = jnp.exp2(so - m)
            l = l + jnp.sum(po, axis=0, keepdims=True)
            p = jnp.concatenate([po.astype(BF16), p], axis=0)
        on = jnp.dot(vt_ref[hh, :, 0:lo + tq], p, preferred_element_type=F32) / l
        o = (on[:, 0:tq] - lam * on[:, tq:2 * tq]).T
        ms_o = jnp.mean(o * o, axis=-1, keepdims=True)
        o = o * lax.rsqrt(ms_o + EPS) * sub_ref[...] * (1.0 - LAM_INIT)
        o_ref[lo:lo + tq, hcols] = (o * g_ref[lo:lo + tq, hcols].astype(F32)).astype(BF16)

    order = [(hh, qi) for hh in range(ATT_HEADS) for qi in reversed(range(s_len // tq))]
    return scores, finish, order


def _mixer_kernel(rq_ref, rk_ref, rv_ref, rg_ref, gn_ref, dmat_ref, xi_ref, zeta_ref, gdec_ref,
                  q_ref, k_ref, v_ref, g_ref, bias_ref, lq1_ref, lk1_ref, lq2_ref, lk2_ref, sub_ref,
                  wo_ref, wg_ref, wp_ref,
                  ro_ref, do_ref, wo_b_ref, wg_b_ref, wp_b_ref, vt_ref):
    wo_b_ref[...] = wo_ref[...].astype(BF16)
    wg_b_ref[...] = wg_ref[...].astype(BF16)
    wp_b_ref[...] = wp_ref[...].astype(BF16)

    r_front, r_back, r_advance, r_steps = _retention_stages(
        rq_ref, rk_ref, rv_ref, rg_ref, gn_ref, dmat_ref, xi_ref, zeta_ref, gdec_ref, ro_ref)
    a_scores, a_finish, order = _attention_stages(
        q_ref, k_ref, v_ref, g_ref, bias_ref, lq1_ref, lk1_ref, lq2_ref, lk2_ref, sub_ref, do_ref, vt_ref)
    every = len(order) // r_steps

    state = jnp.zeros((2 * RET_DK, 2 * RET_DV), F32)
    r_cur = r_front(0)
    a_cur = a_scores(*order[0])
    for n, unit in enumerate(order):
        a_nxt = a_scores(*order[n + 1]) if n + 1 < len(order) else None
        if n % every == 0:
            r_n = n // every
            r_nxt = r_front(r_n + 1) if r_n + 1 < r_steps else None
            kv, scs = r_cur
            r_back(r_n, scs, state)
            state = r_advance(state, kv)
            r_cur = r_nxt
        a_finish(*unit, *a_cur)
        a_cur = a_nxt


def _out_kernel(x_ref, ro_ref, do_ref, p_ref, wo_ref, pn_ref, wg_ref, wp_ref, o_ref):
    tm = x_ref.shape[0]

    def residual(rows):
        acc = jnp.dot(ro_ref[rows, :], wo_ref[0:RET_WIDTH, :], preferred_element_type=F32)
        acc = acc + jnp.dot(do_ref[rows, :], wo_ref[RET_WIDTH:, :], preferred_element_type=F32)
        h1 = x_ref[rows, :] + acc
        ms = jnp.mean(h1 * h1, axis=-1, keepdims=True)
        return h1, (h1 * lax.rsqrt(ms + EPS) * pn_ref[...]).astype(BF16)

    def gated(rows, h1, hn):
        z = jnp.dot(hn, wg_ref[...], preferred_element_type=F32)
        gate = 1.0 / (1.0 + jnp.exp(-z))
        ple = jnp.dot(p_ref[rows, :].astype(BF16), wp_ref[...], preferred_element_type=F32)
        o_ref[rows, :] = h1 + gate * ple

    chunks = [pl.ds(r * OUT_CHUNK, OUT_CHUNK) for r in range(tm // OUT_CHUNK)]
    cur = residual(chunks[0])
    for n, rows in enumerate(chunks):
        nxt = residual(chunks[n + 1]) if n + 1 < len(chunks) else None
        gated(rows, *cur)
        cur = nxt


def _decay_tables():
    log_g = np.log1p(-np.exp2(-5.0 - np.arange(RET_HEADS, dtype=np.float64)))
    idx = np.arange(RET_T, dtype=np.float64)
    dist = np.abs(idx[:, None] - idx[None, :])
    same_or_earlier = (np.arange(RET_T)[None, :] // CHUNK) <= (np.arange(RET_T)[:, None] // CHUNK)
    dmat = np.where(same_or_earlier[None], np.exp(dist[None] * log_g[:, None, None]), 0.0)
    xi = np.exp((idx + 1.0)[None, :] * log_g[:, None])
    zeta = np.exp((RET_T - 1.0 - idx)[None, :] * log_g[:, None])
    g_step = np.exp(RET_T * log_g)
    n_pair = RET_HEADS // 2
    widen = lambda t: np.broadcast_to(t.reshape(n_pair, 2, RET_T, 1), (n_pair, 2, RET_T, RET_DV)) \
        .transpose(0, 2, 1, 3).reshape(n_pair, RET_T, 2 * RET_DV)
    xi_w = widen(xi)
    zeta_w = widen(zeta)
    rowh = (np.arange(2 * RET_DK) // RET_DK)[:, None]
    colh = (np.arange(2 * RET_DV) // RET_DV)[None, :]
    diag = (rowh == colh).astype(np.float64)
    gp = g_step.reshape(n_pair, 2)
    gq = np.where(colh[None] == 0, gp[:, 0, None, None], gp[:, 1, None, None]) * diag[None]
    gdec = np.stack([gq, np.broadcast_to(diag[None], gq.shape)], axis=1)
    return tuple(jnp.asarray(t.astype(np.float32)) for t in (dmat, xi_w, zeta_w, gdec))


def _rope_tables(s_len):
    pos = np.arange(s_len, dtype=np.float64)
    inv_freq = ROPE_BASE ** (-np.arange(RET_DK // 2, dtype=np.float64) / (RET_DK // 2))
    ang = pos[:, None] * inv_freq[None, :]
    cos, sin = np.cos(ang), np.sin(ang)
    cos_t = np.tile(cos, (1, LANES // (RET_DK // 2)))
    sin_t = np.tile(np.concatenate([-sin, sin], axis=-1), (1, LANES // RET_DK))
    return jnp.asarray(cos_t.astype(np.float32)), jnp.asarray(sin_t.astype(np.float32))


def kernel(x, p, attn_norm, w_in, ret_gn, diff_qn, diff_kn, diff_lq1, diff_lk1, diff_lq2, diff_lk2, diff_subln, w_out, ple_norm, w_ple_gate, w_ple_proj):
    b, s, d = x.shape
    m = b * s
    assert d == D_MODEL and s % PROJ_TM == 0 and m % OUT_TM == 0 and s % RET_T == 0 and s % ATT_TQ == 0
    x2 = x.reshape(m, d)
    p2 = p[0].reshape(m, PLE_DIM)

    cos_t, sin_t = _rope_tables(s)
    dmat, xi_w, zeta_w, gdec = _decay_tables()
    qn_t = jnp.tile(diff_qn[0], LANES // DIFF_DH).reshape(1, LANES)
    kn_t = jnp.tile(diff_kn[0], LANES // DIFF_DH).reshape(1, LANES)

    params = functools.partial(pltpu.CompilerParams, vmem_limit_bytes=VMEM_LIMIT_BYTES)
    const = lambda *_: (0, 0)
    pos_blocks = s // PROJ_TM

    n_seg = D_IN // PROJ_TN
    n_row = m // PROJ_TM
    z = pl.pallas_call(
        _proj_kernel,
        grid=(n_row, n_seg),
        in_specs=[
            pl.BlockSpec((PROJ_TM, d // 2), lambda i, j: (jnp.minimum(i + jnp.where(j >= 2, 1, 0), n_row - 1), 0)),
            pl.BlockSpec((PROJ_TM, d // 2), lambda i, j: (jnp.minimum(i + jnp.where(j >= 4, 1, 0), n_row - 1), 1)),
            pl.BlockSpec((1, d), const),
            pl.BlockSpec((d, PROJ_TN), lambda i, j: (0, j)),
            pl.BlockSpec((PROJ_TM, LANES), lambda i, j: (i % pos_blocks, 0)),
            pl.BlockSpec((PROJ_TM, LANES), lambda i, j: (i % pos_blocks, 0)),
            pl.BlockSpec((1, LANES), const),
            pl.BlockSpec((1, LANES), const),
        ],
        out_specs=pl.BlockSpec((None, PROJ_TM, PROJ_TN), lambda i, j: (j, i, 0)),
        out_shape=jax.ShapeDtypeStruct((n_seg, m, PROJ_TN), BF16),
        scratch_shapes=[pltpu.VMEM((PROJ_TM, d), BF16), pltpu.VMEM((d, PROJ_TN), BF16)],
        compiler_params=params(dimension_semantics=("arbitrary", "arbitrary")),
        name="proj",
    )(x2, x2, attn_norm[0].reshape(1, d), w_in[0], cos_t, sin_t, qn_t, kn_t)

    z4 = z.reshape(n_seg, b, s, PROJ_TN)

    def z_block(col0, width):
        seg, blk0 = col0 // PROJ_TN, (col0 % PROJ_TN) // width
        return pl.BlockSpec((None, None, s, width), lambda bi, h: (seg, bi, 0, blk0 + h))
    vec = lambda a: a[0].reshape(1, -1)
    tile_chunk = np.arange(ATT_TK) // CHUNK
    diag_bias = np.where(tile_chunk[:, None] <= tile_chunk[None, :], 0.0, NEG_BIG).astype(np.float32)
    diag_bias = jnp.asarray(np.tile(diag_bias, (1, 2)))
    hw = ATT_HEADS * LANES
    assert (DQ_BLK % ATT_HEADS, DK_BLK % ATT_HEADS, DV_BLK % ATT_HEADS, DG_BLK % ATT_HEADS) == (0, 0, 0, 0)
    n_att = DIFF_HEADS // ATT_HEADS
    assert n_att == RET_HEADS // 2
    w_slice = lambda rows: pl.BlockSpec((rows // (b * n_att), d), lambda bi, h: (bi * n_att + h, 0))
    ro, do, wo_b, wg_b, wp_b = pl.pallas_call(
        _mixer_kernel,
        grid=(b, n_att),
        in_specs=[
            z_block(RQ_BLK * LANES, LANES),
            z_block(RK_BLK * LANES, LANES),
            z_block(RV_COL, 2 * RET_DV),
            z_block(RG_COL, 2 * RET_DV),
            pl.BlockSpec((1, 2 * RET_DV), lambda bi, hp: (0, hp)),
            pl.BlockSpec((2, RET_T, RET_T), lambda bi, hp: (hp, 0, 0)),
            pl.BlockSpec((None, RET_T, 2 * RET_DV), lambda bi, hp: (hp, 0, 0)),
            pl.BlockSpec((None, RET_T, 2 * RET_DV), lambda bi, hp: (hp, 0, 0)),
            pl.BlockSpec((None, 2, 2 * RET_DK, 2 * RET_DV), lambda bi, hp: (hp, 0, 0, 0)),
            z_block(DQ_BLK * LANES, hw),
            z_block(DK_BLK * LANES, hw),
            z_block(DV_BLK * LANES, hw),
            z_block(DG_BLK * LANES, hw),
            pl.BlockSpec((ATT_TK, 2 * ATT_TQ), lambda *_: (0, 0)),
            pl.BlockSpec((1, DIFF_DH), lambda *_: (0, 0)),
            pl.BlockSpec((1, DIFF_DH), lambda *_: (0, 0)),
            pl.BlockSpec((1, DIFF_DH), lambda *_: (0, 0)),
            pl.BlockSpec((1, DIFF_DH), lambda *_: (0, 0)),
            pl.BlockSpec((1, DIFF_DV), lambda *_: (0, 0)),
            w_slice(d), w_slice(d), w_slice(PLE_DIM),
        ],
        out_specs=[pl.BlockSpec((None, s, 2 * RET_DV), lambda bi, hp: (bi, 0, hp)),
                   pl.BlockSpec((None, s, hw), lambda bi, h: (bi, 0, h)),
                   w_slice(d), w_slice(d), w_slice(PLE_DIM)],
        out_shape=[jax.ShapeDtypeStruct((b, s, RET_WIDTH), BF16),
                   jax.ShapeDtypeStruct((b, s, DIFF_WIDTH), BF16),
                   jax.ShapeDtypeStruct((d, d), BF16), jax.ShapeDtypeStruct((d, d), BF16),
                   jax.ShapeDtypeStruct((PLE_DIM, d), BF16)],
        scratch_shapes=[pltpu.VMEM((ATT_HEADS, DIFF_DV, s), BF16)],
        compiler_params=params(dimension_semantics=("arbitrary", "arbitrary")),
        name="mixer",
    )(z4, z4, z4, z4, ret_gn[0].reshape(1, RET_WIDTH), dmat, xi_w, zeta_w, gdec,
      z4, z4, z4, z4, diag_bias, vec(diff_lq1), vec(diff_lk1), vec(diff_lq2), vec(diff_lk2), vec(diff_subln),
      w_out[0], w_ple_gate[0], w_ple_proj[0])

    resident = functools.partial(pl.BlockSpec, pipeline_mode=pl.Buffered(1))
    out = pl.pallas_call(
        _out_kernel,
        grid=(m // OUT_TM,),
        in_specs=[
            pl.BlockSpec((OUT_TM, d), lambda i: (i, 0)),
            pl.BlockSpec((OUT_TM, RET_WIDTH), lambda i: (i, 0)),
            pl.BlockSpec((OUT_TM, DIFF_WIDTH), lambda i: (i, 0)),
            pl.BlockSpec((OUT_TM, PLE_DIM), lambda i: (i, 0)),
            resident((d, d), lambda i: (0, 0)),
            pl.BlockSpec((1, d), lambda i: (0, 0)),
            resident((d, d), lambda i: (0, 0)),
            resident((PLE_DIM, d), lambda i: (0, 0)),
        ],
        out_specs=pl.BlockSpec((OUT_TM, d), lambda i: (i, 0)),
        out_shape=jax.ShapeDtypeStruct((m, d), F32),
        compiler_params=params(dimension_semantics=("arbitrary",)),
        name="outproj",
    )(x2, ro.reshape(m, RET_WIDTH), do.reshape(m, DIFF_WIDTH), p2, wo_b, ple_norm[0].reshape(1, d), wg_b, wp_b)

    return out.reshape(b, s, d)
```

```python
import functools
import math

import jax
import jax.numpy as jnp
import numpy as np
from jax import lax
from jax.experimental import pallas as pl
from jax.experimental.pallas import tpu as pltpu

F32 = jnp.float32
BF16 = jnp.bfloat16

D_MODEL = 2048
CHUNK = 64
PLE_DIM = 256
RET_WIDTH = 1024
DIFF_WIDTH = 1024
RET_HEADS = 8
RET_DV = 128
RET_DK = 64
DIFF_HEADS = 8
DIFF_DV = 128
DIFF_DH = 64
ROPE_BASE = 10000.0
EPS = 1e-6
D_IN = 7168
LAM_INIT = 0.8 - 0.6 * math.exp(-0.3 * 0)

LANES = 128
VMEM_LIMIT_BYTES = 56 * 1024 * 1024

RQ_BLK = 0
RK_BLK = 512 // LANES
RV_COL = 1024
RG_COL = 2048
DQ_BLK = 3072 // LANES
DK_BLK = 4096 // LANES
DV_BLK = 5120 // LANES
DG_BLK = 6144 // LANES

PROJ_TM = 1024
PROJ_TN = 1024
ROW_CHUNK = 256
RET_T = 256
ATT_TQ = 256
ATT_TK = 256
ATT_HEADS = 2
OUT_TM = 512
OUT_CHUNK = 256
LOG2E = math.log2(math.e)
NEG_BIG = -1e30


def _silu(a):
    return a * (1.0 / (1.0 + jnp.exp(-a)))


def _swap_halves(a):
    lane = lax.broadcasted_iota(jnp.int32, a.shape, 1)
    first = (lane % RET_DK) < (RET_DK // 2)
    return jnp.where(first, pltpu.roll(a, LANES - RET_DK // 2, 1), pltpu.roll(a, RET_DK // 2, 1))


def _proj_kernel(xa_ref, xb_ref, g_ref, wl_ref, wr_ref, cos_ref, sin_ref, qn_ref, kn_ref, o_ref, u_ref, wb_ref):
    j = pl.program_id(1)
    tm = xa_ref.shape[0]
    half = xa_ref.shape[1]
    wn = wl_ref.shape[1]
    tn = 2 * wn

    def rope_epilogue(a, rows):
        c = cos_ref[rows, :]
        s = sin_ref[rows, :]
        for sl in range(tn // LANES):
            scale = 1.0 if sl * LANES < RET_HEADS * RET_DK else RET_DK ** -0.5
            asl = a[:, sl * LANES:(sl + 1) * LANES]
            o_ref[rows, pl.ds(sl * LANES, LANES)] = ((asl * c + _swap_halves(asl) * s) * scale).astype(BF16)

    def norm_epilogue(gn_ref, scale):
        def apply(a, rows):
            low = lax.broadcasted_iota(jnp.int32, (1, LANES), 1) < DIFF_DH
            for sl in range(tn // LANES):
                asl = a[:, sl * LANES:(sl + 1) * LANES]
                sq = asl * asl
                s_lo = jnp.sum(jnp.where(low, sq, 0.0), axis=-1, keepdims=True)
                s_hi = jnp.sum(jnp.where(low, 0.0, sq), axis=-1, keepdims=True)
                ms = jnp.where(low, s_lo, s_hi) * (1.0 / DIFF_DH)
                o_ref[rows, pl.ds(sl * LANES, LANES)] = (asl * lax.rsqrt(ms + EPS) * gn_ref[...] * scale).astype(BF16)
        return apply

    def silu_epilogue(a, rows):
        o_ref[rows, :] = _silu(a).astype(BF16)

    def plain_epilogue(a, rows):
        o_ref[rows, :] = a.astype(BF16)

    def run(epilogue, normalize_input=False):
        sizes = [ROW_CHUNK] * (tm // ROW_CHUNK - 1) + [ROW_CHUNK // 2] * 2
        for kt in range(wl_ref.shape[0] // ROW_CHUNK):
            krows = pl.ds(kt * ROW_CHUNK, ROW_CHUNK)
            wb_ref[krows, 0:wn] = wl_ref[krows, :].astype(BF16)
            wb_ref[krows, wn:tn] = wr_ref[krows, :].astype(BF16)
        for r, size in enumerate(sizes):
            rows = pl.ds(sum(sizes[:r]), size)
            if normalize_input:
                xa = xa_ref[rows, :]
                xb = xb_ref[rows, :]
                ss = jnp.sum(xa * xa, axis=-1, keepdims=True) + jnp.sum(xb * xb, axis=-1, keepdims=True)
                inv = lax.rsqrt(ss * (1.0 / (2 * half)) + EPS)
                u_ref[rows, 0:half] = (xa * inv * g_ref[:, 0:half]).astype(BF16)
                u_ref[rows, half:2 * half] = (xb * inv * g_ref[:, half:2 * half]).astype(BF16)
            epilogue(jnp.dot(u_ref[rows, :], wb_ref[...], preferred_element_type=F32), rows)

    @pl.when(j == 0)
    def _():
        run(rope_epilogue, normalize_input=True)

    @pl.when((j == 1) | (j == 5))
    def _():
        run(plain_epilogue)

    @pl.when((j == 2) | (j == 6))
    def _():
        run(silu_epilogue)

    @pl.when(j == 3)
    def _():
        run(norm_epilogue(qn_ref, DIFF_DH ** -0.5 * LOG2E))

    @pl.when(j == 4)
    def _():
        run(norm_epilogue(kn_ref, 1.0))


def _retention_stages(q_ref, k_ref, v_ref, g_ref, gn_ref, dmat_ref, xi_ref, zeta_ref, gdec_ref, o_ref):
    s_len = q_ref.shape[0]
    lane = lax.broadcasted_iota(jnp.int32, (1, LANES), 1)

    def front(n):
        rows = pl.ds(n * RET_T, RET_T)
        q = q_ref[rows, :]
        k = k_ref[rows, :]
        vz = (v_ref[rows, :].astype(F32) * zeta_ref[...]).astype(BF16)
        kv = lax.dot_general(k, vz, (((0,), (0,)), ((), ())), preferred_element_type=F32)
        scs = []
        for hh in range(2):
            qm = jnp.where((lane < RET_DK) == (hh == 0), q, jnp.zeros_like(q))
            sc = lax.dot_general(qm, k, (((1,), (1,)), ((), ())), preferred_element_type=F32)
            scs.append((sc * dmat_ref[hh]).astype(BF16))
        return kv, scs

    def back(n, scs, state):
        rows = pl.ds(n * RET_T, RET_T)
        cross = jnp.dot(q_ref[rows, :], state.astype(BF16), preferred_element_type=F32)
        for hh in range(2):
            cols = pl.ds(hh * RET_DV, RET_DV)
            intra = jnp.dot(scs[hh], v_ref[rows, cols], preferred_element_type=F32)
            o = intra + cross[:, hh * RET_DV:(hh + 1) * RET_DV] * xi_ref[:, cols]
            ms = jnp.mean(o * o, axis=-1, keepdims=True)
            o = o * lax.rsqrt(ms + EPS) * gn_ref[:, cols]
            o_ref[rows, cols] = (o * g_ref[rows, cols].astype(F32)).astype(BF16)

    def advance(state, kv):
        return state * gdec_ref[0] + kv * gdec_ref[1]

    return front, back, advance, s_len // RET_T


def _attention_stages(q_ref, k_ref, v_ref, g_ref, bias_ref, lq1_ref, lk1_ref, lq2_ref, lk2_ref, sub_ref, o_ref,
                      vt_ref):
    s_len = k_ref.shape[0]
    tq = ATT_TQ
    for hh in range(ATT_HEADS):
        for t in range(s_len // ATT_TK):
            cols = pl.ds(t * ATT_TK, ATT_TK)
            vt_ref[hh, :, cols] = v_ref[cols, hh * DIFF_DV:(hh + 1) * DIFF_DV].astype(F32).T.astype(BF16)

    lam = (jnp.exp(jnp.sum(lq1_ref[...] * lk1_ref[...], axis=-1, keepdims=True))
           - jnp.exp(jnp.sum(lq2_ref[...] * lk2_ref[...], axis=-1, keepdims=True)) + LAM_INIT)

    def scores(hh, qi):
        lo = qi * tq
        hcols = slice(hh * LANES, (hh + 1) * LANES)
        qt = q_ref[lo:lo + tq, hcols].astype(F32).T
        row = lax.broadcasted_iota(jnp.int32, qt.shape, 0)
        qbd = jnp.concatenate([jnp.where(row < DIFF_DH, qt, 0.0), jnp.where(row >= DIFF_DH, qt, 0.0)],
                              axis=1).astype(BF16)
        sd = jnp.dot(k_ref[lo:lo + tq, hcols], qbd, preferred_element_type=F32) + bias_ref[...]
        m = jnp.max(sd, axis=0, keepdims=True)
        so = None
        if qi > 0:
            so = jnp.dot(k_ref[0:lo, hcols], qbd, preferred_element_type=F32)
            m = jnp.maximum(m, jnp.max(so, axis=0, keepdims=True))
        return sd, so, m

    def finish(hh, qi, sd, so, m):
        lo = qi * tq
        hcols = slice(hh * LANES, (hh + 1) * LANES)
        pd = jnp.exp2(sd - m)
        l = jnp.sum(pd, axis=0, keepdims=True)
        p = pd.astype(BF16)
        if qi > 0:
            po = jnp.exp2(so - m)
            l = l + jnp.sum(po, axis=0, keepdims=True)
            p = jnp.concatenate([po.astype(BF16), p], axis=0)
        on = jnp.dot(vt_ref[hh, :, 0:lo + tq], p, preferred_element_type=F32) / l
        o = (on[:, 0:tq] - lam * on[:, tq:2 * tq]).T
        ms_o = jnp.mean(o * o, axis=-1, keepdims=True)
        o = o * lax.rsqrt(ms_o + EPS) * sub_ref[...] * (1.0 - LAM_INIT)
        o_ref[lo:lo + tq, hcols] = (o * g_ref[lo:lo + tq, hcols].astype(F32)).astype(BF16)

    order = [(hh, qi) for hh in range(ATT_HEADS) for qi in reversed(range(s_len // tq))]
    return scores, finish, order


def _mixer_kernel(rq_ref, rk_ref, rv_ref, rg_ref, gn_ref, dmat_ref, xi_ref, zeta_ref, gdec_ref,
                  q_ref, k_ref, v_ref, g_ref, bias_ref, lq1_ref, lk1_ref, lq2_ref, lk2_ref, sub_ref,
                  wo_ref, wg_ref, wp_ref,
                  ro_ref, do_ref, wo_b_ref, wg_b_ref, wp_b_ref, vt_ref):
    wo_b_ref[...] = wo_ref[...].astype(BF16)
    wg_b_ref[...] = wg_ref[...].astype(BF16)
    wp_b_ref[...] = wp_ref[...].astype(BF16)

    r_front, r_back, r_advance, r_steps = _retention_stages(
        rq_ref, rk_ref, rv_ref, rg_ref, gn_ref, dmat_ref, xi_ref, zeta_ref, gdec_ref, ro_ref)
    a_scores, a_finish, order = _attention_stages(
        q_ref, k_ref, v_ref, g_ref, bias_ref, lq1_ref, lk1_ref, lq2_ref, lk2_ref, sub_ref, do_ref, vt_ref)
    every = len(order) // r_steps

    state = jnp.zeros((2 * RET_DK, 2 * RET_DV), F32)
    r_cur = r_front(0)
    a_cur = a_scores(*order[0])
    for n, unit in enumerate(order):
        a_nxt = a_scores(*order[n + 1]) if n + 1 < len(order) else None
        if n % every == 0:
            r_n = n // every
            r_nxt = r_front(r_n + 1) if r_n + 1 < r_steps else None
            kv, scs = r_cur
            r_back(r_n, scs, state)
            state = r_advance(state, kv)
            r_cur = r_nxt
        a_finish(*unit, *a_cur)
        a_cur = a_nxt


def _out_kernel(x_ref, ro_ref, do_ref, p_ref, wo_ref, pn_ref, wg_ref, wp_ref, o_ref):
    tm = x_ref.shape[0]

    def residual(rows):
        acc = jnp.dot(ro_ref[rows, :], wo_ref[0:RET_WIDTH, :], preferred_element_type=F32)
        acc = acc + jnp.dot(do_ref[rows, :], wo_ref[RET_WIDTH:, :], preferred_element_type=F32)
        h1 = x_ref[rows, :] + acc
        ms = jnp.mean(h1 * h1, axis=-1, keepdims=True)
        return h1, (h1 * lax.rsqrt(ms + EPS) * pn_ref[...]).astype(BF16)

    def gated(rows, h1, hn):
        z = jnp.dot(hn, wg_ref[...], preferred_element_type=F32)
        gate = 1.0 / (1.0 + jnp.exp(-z))
        ple = jnp.dot(p_ref[rows, :].astype(BF16), wp_ref[...], preferred_element_type=F32)
        o_ref[rows, :] = h1 + gate * ple

    chunks = [pl.ds(r * OUT_CHUNK, OUT_CHUNK) for r in range(tm // OUT_CHUNK)]
    cur = residual(chunks[0])
    for n, rows in enumerate(chunks):
        nxt = residual(chunks[n + 1]) if n + 1 < len(chunks) else None
        gated(rows, *cur)
        cur = nxt


def _decay_tables():
    log_g = np.log1p(-np.exp2(-5.0 - np.arange(RET_HEADS, dtype=np.float64)))
    idx = np.arange(RET_T, dtype=np.float64)
    dist = np.abs(idx[:, None] - idx[None, :])
    same_or_earlier = (np.arange(RET_T)[None, :] // CHUNK) <= (np.arange(RET_T)[:, None] // CHUNK)
    dmat = np.where(same_or_earlier[None], np.exp(dist[None] * log_g[:, None, None]), 0.0)
    xi = np.exp((idx + 1.0)[None, :] * log_g[:, None])
    zeta = np.exp((RET_T - 1.0 - idx)[None, :] * log_g[:, None])
    g_step = np.exp(RET_T * log_g)
    n_pair = RET_HEADS // 2
    widen = lambda t: np.broadcast_to(t.reshape(n_pair, 2, RET_T, 1), (n_pair, 2, RET_T, RET_DV)) \
        .transpose(0, 2, 1, 3).reshape(n_pair, RET_T, 2 * RET_DV)
    xi_w = widen(xi)
    zeta_w = widen(zeta)
    rowh = (np.arange(2 * RET_DK) // RET_DK)[:, None]
    colh = (np.arange(2 * RET_DV) // RET_DV)[None, :]
    diag = (rowh == colh).astype(np.float64)
    gp = g_step.reshape(n_pair, 2)
    gq = np.where(colh[None] == 0, gp[:, 0, None, None], gp[:, 1, None, None]) * diag[None]
    gdec = np.stack([gq, np.broadcast_to(diag[None], gq.shape)], axis=1)
    return tuple(jnp.asarray(t.astype(np.float32)) for t in (dmat, xi_w, zeta_w, gdec))


def _rope_tables(s_len):
    pos = np.arange(s_len, dtype=np.float64)
    inv_freq = ROPE_BASE ** (-np.arange(RET_DK // 2, dtype=np.float64) / (RET_DK // 2))
    ang = pos[:, None] * inv_freq[None, :]
    cos, sin = np.cos(ang), np.sin(ang)
    cos_t = np.tile(cos, (1, LANES // (RET_DK // 2)))
    sin_t = np.tile(np.concatenate([-sin, sin], axis=-1), (1, LANES // RET_DK))
    return jnp.asarray(cos_t.astype(np.float32)), jnp.asarray(sin_t.astype(np.float32))


def kernel(x, p, attn_norm, w_in, ret_gn, diff_qn, diff_kn, diff_lq1, diff_lk1, diff_lq2, diff_lk2, diff_subln, w_out, ple_norm, w_ple_gate, w_ple_proj):
    b, s, d = x.shape
    m = b * s
    assert d == D_MODEL and s % PROJ_TM == 0 and m % OUT_TM == 0 and s % RET_T == 0 and s % ATT_TQ == 0
    x2 = x.reshape(m, d)
    p2 = p[0].reshape(m, PLE_DIM)

    cos_t, sin_t = _rope_tables(s)
    dmat, xi_w, zeta_w, gdec = _decay_tables()
    qn_t = jnp.tile(diff_qn[0], LANES // DIFF_DH).reshape(1, LANES)
    kn_t = jnp.tile(diff_kn[0], LANES // DIFF_DH).reshape(1, LANES)

    params = functools.partial(pltpu.CompilerParams, vmem_limit_bytes=VMEM_LIMIT_BYTES)
    const = lambda *_: (0, 0)
    pos_blocks = s // PROJ_TM

    n_seg = D_IN // PROJ_TN
    n_row = m // PROJ_TM
    z = pl.pallas_call(
        _proj_kernel,
        grid=(n_row, n_seg),
        in_specs=[
            pl.BlockSpec((PROJ_TM, d // 2), lambda i, j: (jnp.minimum(i + jnp.where(j >= 2, 1, 0), n_row - 1), 0)),
            pl.BlockSpec((PROJ_TM, d // 2), lambda i, j: (jnp.minimum(i + jnp.where(j >= 4, 1, 0), n_row - 1), 1)),
            pl.BlockSpec((1, d), const),
            pl.BlockSpec((d, PROJ_TN // 2), lambda i, j: (0, 2 * j)),
            pl.BlockSpec((d, PROJ_TN // 2), lambda i, j: (0, 2 * j + 1)),
            pl.BlockSpec((PROJ_TM, LANES), lambda i, j: ((i + jnp.where(j >= 1, 1, 0)) % pos_blocks, 0)),
            pl.BlockSpec((PROJ_TM, LANES), lambda i, j: ((i + jnp.where(j >= 3, 1, 0)) % pos_blocks, 0)),
            pl.BlockSpec((1, LANES), const),
            pl.BlockSpec((1, LANES), const),
        ],
        out_specs=pl.BlockSpec((None, PROJ_TM, PROJ_TN), lambda i, j: (j, i, 0)),
        out_shape=jax.ShapeDtypeStruct((n_seg, m, PROJ_TN), BF16),
        scratch_shapes=[pltpu.VMEM((PROJ_TM, d), BF16), pltpu.VMEM((d, PROJ_TN), BF16)],
        compiler_params=params(dimension_semantics=("arbitrary", "arbitrary")),
        name="proj",
    )(x2, x2, attn_norm[0].reshape(1, d), w_in[0], w_in[0], cos_t, sin_t, qn_t, kn_t)

    z4 = z.reshape(n_seg, b, s, PROJ_TN)

    def z_block(col0, width):
        seg, blk0 = col0 // PROJ_TN, (col0 % PROJ_TN) // width
        return pl.BlockSpec((None, None, s, width), lambda bi, h: (seg, bi, 0, blk0 + h))
    vec = lambda a: a[0].reshape(1, -1)
    tile_chunk = np.arange(ATT_TK) // CHUNK
    diag_bias = np.where(tile_chunk[:, None] <= tile_chunk[None, :], 0.0, NEG_BIG).astype(np.float32)
    diag_bias = jnp.asarray(np.tile(diag_bias, (1, 2)))
    hw = ATT_HEADS * LANES
    assert (DQ_BLK % ATT_HEADS, DK_BLK % ATT_HEADS, DV_BLK % ATT_HEADS, DG_BLK % ATT_HEADS) == (0, 0, 0, 0)
    n_att = DIFF_HEADS // ATT_HEADS
    assert n_att == RET_HEADS // 2
    w_slice = lambda rows: pl.BlockSpec((rows // (b * n_att), d), lambda bi, h: (bi * n_att + h, 0))
    ro, do, wo_b, wg_b, wp_b = pl.pallas_call(
        _mixer_kernel,
        grid=(b, n_att),
        in_specs=[
            z_block(RQ_BLK * LANES, LANES),
            z_block(RK_BLK * LANES, LANES),
            z_block(RV_COL, 2 * RET_DV),
            z_block(RG_COL, 2 * RET_DV),
            pl.BlockSpec((1, 2 * RET_DV), lambda bi, hp: (0, hp)),
            pl.BlockSpec((2, RET_T, RET_T), lambda bi, hp: (hp, 0, 0)),
            pl.BlockSpec((None, RET_T, 2 * RET_DV), lambda bi, hp: (hp, 0, 0)),
            pl.BlockSpec((None, RET_T, 2 * RET_DV), lambda bi, hp: (hp, 0, 0)),
            pl.BlockSpec((None, 2, 2 * RET_DK, 2 * RET_DV), lambda bi, hp: (hp, 0, 0, 0)),
            z_block(DQ_BLK * LANES, hw),
            z_block(DK_BLK * LANES, hw),
            z_block(DV_BLK * LANES, hw),
            z_block(DG_BLK * LANES, hw),
            pl.BlockSpec((ATT_TK, 2 * ATT_TQ), lambda *_: (0, 0)),
            pl.BlockSpec((1, DIFF_DH), lambda *_: (0, 0)),
            pl.BlockSpec((1, DIFF_DH), lambda *_: (0, 0)),
            pl.BlockSpec((1, DIFF_DH), lambda *_: (0, 0)),
            pl.BlockSpec((1, DIFF_DH), lambda *_: (0, 0)),
            pl.BlockSpec((1, DIFF_DV), lambda *_: (0, 0)),
            w_slice(d), w_slice(d), w_slice(PLE_DIM),
        ],
        out_specs=[pl.BlockSpec((None, s, 2 * RET_DV), lambda bi, hp: (bi, 0, hp)),
                   pl.BlockSpec((None, s, hw), lambda bi, h: (bi, 0, h)),
                   w_slice(d), w_slice(d), w_slice(PLE_DIM)],
        out_shape=[jax.ShapeDtypeStruct((b, s, RET_WIDTH), BF16),
                   jax.ShapeDtypeStruct((b, s, DIFF_WIDTH), BF16),
                   jax.ShapeDtypeStruct((d, d), BF16), jax.ShapeDtypeStruct((d, d), BF16),
                   jax.ShapeDtypeStruct((PLE_DIM, d), BF16)],
        scratch_shapes=[pltpu.VMEM((ATT_HEADS, DIFF_DV, s), BF16)],
        compiler_params=params(dimension_semantics=("arbitrary", "arbitrary")),
        name="mixer",
    )(z4, z4, z4, z4, ret_gn[0].reshape(1, RET_WIDTH), dmat, xi_w, zeta_w, gdec,
      z4, z4, z4, z4, diag_bias, vec(diff_lq1), vec(diff_lk1), vec(diff_lq2), vec(diff_lk2), vec(diff_subln),
      w_out[0], w_ple_gate[0], w_ple_proj[0])

    resident = functools.partial(pl.BlockSpec, pipeline_mode=pl.Buffered(1))
    out = pl.pallas_call(
        _out_kernel,
        grid=(m // OUT_TM,),
        in_specs=[
            pl.BlockSpec((OUT_TM, d), lambda i: (i, 0)),
            pl.BlockSpec((OUT_TM, RET_WIDTH), lambda i: (i, 0)),
            pl.BlockSpec((OUT_TM, DIFF_WIDTH), lambda i: (i, 0)),
            pl.BlockSpec((OUT_TM, PLE_DIM), lambda i: (i, 0)),
            resident((d, d), lambda i: (0, 0)),
            pl.BlockSpec((1, d), lambda i: (0, 0)),
            resident((d, d), lambda i: (0, 0)),
            resident((PLE_DIM, d), lambda i: (0, 0)),
        ],
        out_specs=pl.BlockSpec((OUT_TM, d), lambda i: (i, 0)),
        out_shape=jax.ShapeDtypeStruct((m, d), F32),
        compiler_params=params(dimension_semantics=("arbitrary",)),
        name="outproj",
    )(x2, ro.reshape(m, RET_WIDTH), do.reshape(m, DIFF_WIDTH), p2, wo_b, ple_norm[0].reshape(1, d), wg_b, wp_b)

    return out.reshape(b, s, d)
```

```python
import functools
import math

import jax
import jax.numpy as jnp
import numpy as np
from jax import lax
from jax.experimental import pallas as pl
from jax.experimental.pallas import tpu as pltpu

F32 = jnp.float32
BF16 = jnp.bfloat16

D_MODEL = 2048
CHUNK = 64
PLE_DIM = 256
RET_WIDTH = 1024
DIFF_WIDTH = 1024
RET_HEADS = 8
RET_DV = 128
RET_DK = 64
DIFF_HEADS = 8
DIFF_DV = 128
DIFF_DH = 64
ROPE_BASE = 10000.0
EPS = 1e-6
D_IN = 7168
LAM_INIT = 0.8 - 0.6 * math.exp(-0.3 * 0)

LANES = 128
VMEM_LIMIT_BYTES = 56 * 1024 * 1024

RQ_BLK = 0
RK_BLK = 512 // LANES
RV_COL = 1024
RG_COL = 2048
DQ_BLK = 3072 // LANES
DK_BLK = 4096 // LANES
DV_BLK = 5120 // LANES
DG_BLK = 6144 // LANES

PROJ_TM = 1024
PROJ_TN = 1024
ROW_CHUNK = 256
RET_T = 256
ATT_TQ = 256
ATT_TK = 256
ATT_HEADS = 2
OUT_TM = 512
OUT_CHUNK = 256
LOG2E = math.log2(math.e)
NEG_BIG = -1e30


def _silu(a):
    return a * (1.0 / (1.0 + jnp.exp(-a)))


def _swap_halves(a):
    lane = lax.broadcasted_iota(jnp.int32, a.shape, 1)
    first = (lane % RET_DK) < (RET_DK // 2)
    return jnp.where(first, pltpu.roll(a, LANES - RET_DK // 2, 1), pltpu.roll(a, RET_DK // 2, 1))


def _proj_kernel(xa_ref, xb_ref, g_ref, wl_ref, wr_ref, cos_ref, sin_ref, qn_ref, kn_ref, o_ref, wh_ref,
                 u_ref, wb_ref, rsem, wsem):
    i = pl.program_id(0)
    j = pl.program_id(1)
    n_seg = pl.num_programs(1)
    t = i * n_seg + j
    slot = lax.rem(t, 2)
    other = 1 - slot
    tm = xa_ref.shape[0]
    half = xa_ref.shape[1]
    wn = wl_ref.shape[1]
    tn = 2 * wn

    def wh_block(seg):
        return wh_ref.at[:, pl.ds(pl.multiple_of(seg * tn, tn), tn)]

    def write_copy(sl, seg):
        return pltpu.make_async_copy(wb_ref.at[sl], wh_block(seg), wsem.at[sl])

    def read_copy(sl, seg):
        return pltpu.make_async_copy(wh_block(seg), wb_ref.at[sl], rsem.at[sl])

    @pl.when(i == 0)
    def _():
        @pl.when(t >= 2)
        def _():
            write_copy(slot, j - 2).wait()

        for kt in range(wl_ref.shape[0] // ROW_CHUNK):
            krows = pl.ds(kt * ROW_CHUNK, ROW_CHUNK)
            wb_ref[slot, krows, 0:wn] = wl_ref[krows, :].astype(BF16)
            wb_ref[slot, krows, wn:tn] = wr_ref[krows, :].astype(BF16)
        write_copy(slot, j).start()

    @pl.when(i > 0)
    def _():
        read_copy(slot, j).wait()

    @pl.when((t + 1 >= n_seg) & (t + 1 < pl.num_programs(0) * n_seg))
    def _():
        @pl.when((t >= 1) & (t - 1 < n_seg))
        def _():
            write_copy(other, lax.rem(t - 1, n_seg)).wait()

        read_copy(other, lax.rem(j + 1, n_seg)).start()

    def rope_epilogue(a, rows):
        c = cos_ref[rows, :]
        s = sin_ref[rows, :]
        for sl in range(tn // LANES):
            scale = 1.0 if sl * LANES < RET_HEADS * RET_DK else RET_DK ** -0.5
            asl = a[:, sl * LANES:(sl + 1) * LANES]
            o_ref[rows, pl.ds(sl * LANES, LANES)] = ((asl * c + _swap_halves(asl) * s) * scale).astype(BF16)

    def norm_epilogue(gn_ref, scale):
        def apply(a, rows):
            low = lax.broadcasted_iota(jnp.int32, (1, LANES), 1) < DIFF_DH
            for sl in range(tn // LANES):
                asl = a[:, sl * LANES:(sl + 1) * LANES]
                sq = asl * asl
                s_lo = jnp.sum(jnp.where(low, sq, 0.0), axis=-1, keepdims=True)
                s_hi = jnp.sum(jnp.where(low, 0.0, sq), axis=-1, keepdims=True)
                ms = jnp.where(low, s_lo, s_hi) * (1.0 / DIFF_DH)
                o_ref[rows, pl.ds(sl * LANES, LANES)] = (asl * lax.rsqrt(ms + EPS) * gn_ref[...] * scale).astype(BF16)
        return apply

    def silu_epilogue(a, rows):
        o_ref[rows, :] = _silu(a).astype(BF16)

    def plain_epilogue(a, rows):
        o_ref[rows, :] = a.astype(BF16)

    def run(epilogue, normalize_input=False):
        sizes = [ROW_CHUNK] * (tm // ROW_CHUNK - 1) + [ROW_CHUNK // 2] * 2
        for r, size in enumerate(sizes):
            rows = pl.ds(sum(sizes[:r]), size)
            if normalize_input:
                xa = xa_ref[rows, :]
                xb = xb_ref[rows, :]
                ss = jnp.sum(xa * xa, axis=-1, keepdims=True) + jnp.sum(xb * xb, axis=-1, keepdims=True)
                inv = lax.rsqrt(ss * (1.0 / (2 * half)) + EPS)
                u_ref[rows, 0:half] = (xa * inv * g_ref[:, 0:half]).astype(BF16)
                u_ref[rows, half:2 * half] = (xb * inv * g_ref[:, half:2 * half]).astype(BF16)
            epilogue(jnp.dot(u_ref[rows, :], wb_ref[slot], preferred_element_type=F32), rows)

    @pl.when(j == 0)
    def _():
        run(rope_epilogue, normalize_input=True)

    @pl.when((j == 1) | (j == 5))
    def _():
        run(plain_epilogue)

    @pl.when((j == 2) | (j == 6))
    def _():
        run(silu_epilogue)

    @pl.when(j == 3)
    def _():
        run(norm_epilogue(qn_ref, DIFF_DH ** -0.5 * LOG2E))

    @pl.when(j == 4)
    def _():
        run(norm_epilogue(kn_ref, 1.0))


def _retention_stages(q_ref, k_ref, v_ref, g_ref, gn_ref, dmat_ref, xi_ref, zeta_ref, gdec_ref, o_ref):
    s_len = q_ref.shape[0]
    lane = lax.broadcasted_iota(jnp.int32, (1, LANES), 1)

    def front(n):
        rows = pl.ds(n * RET_T, RET_T)
        q = q_ref[rows, :]
        k = k_ref[rows, :]
        vz = (v_ref[rows, :].astype(F32) * zeta_ref[...]).astype(BF16)
        kv = lax.dot_general(k, vz, (((0,), (0,)), ((), ())), preferred_element_type=F32)
        scs = []
        for hh in range(2):
            qm = jnp.where((lane < RET_DK) == (hh == 0), q, jnp.zeros_like(q))
            sc = lax.dot_general(qm, k, (((1,), (1,)), ((), ())), preferred_element_type=F32)
            scs.append((sc * dmat_ref[hh]).astype(BF16))
        return kv, scs

    def back(n, scs, state):
        rows = pl.ds(n * RET_T, RET_T)
        cross = jnp.dot(q_ref[rows, :], state.astype(BF16), preferred_element_type=F32)
        for hh in range(2):
            cols = pl.ds(hh * RET_DV, RET_DV)
            intra = jnp.dot(scs[hh], v_ref[rows, cols], preferred_element_type=F32)
            o = intra + cross[:, hh * RET_DV:(hh + 1) * RET_DV] * xi_ref[:, cols]
            ms = jnp.mean(o * o, axis=-1, keepdims=True)
            o = o * lax.rsqrt(ms + EPS) * gn_ref[:, cols]
            o_ref[rows, cols] = (o * g_ref[rows, cols].astype(F32)).astype(BF16)

    def advance(state, kv):
        return state * gdec_ref[0] + kv * gdec_ref[1]

    return front, back, advance, s_len // RET_T


def _attention_stages(q_ref, k_ref, v_ref, g_ref, bias_ref, lq1_ref, lk1_ref, lq2_ref, lk2_ref, sub_ref, o_ref,
                      vt_ref):
    s_len = k_ref.shape[0]
    tq = ATT_TQ
    for hh in range(ATT_HEADS):
        for t in range(s_len // ATT_TK):
            cols = pl.ds(t * ATT_TK, ATT_TK)
            vt_ref[hh, :, cols] = v_ref[cols, hh * DIFF_DV:(hh + 1) * DIFF_DV].astype(F32).T.astype(BF16)

    lam = (jnp.exp(jnp.sum(lq1_ref[...] * lk1_ref[...], axis=-1, keepdims=True))
           - jnp.exp(jnp.sum(lq2_ref[...] * lk2_ref[...], axis=-1, keepdims=True)) + LAM_INIT)

    def scores(hh, qi):
        lo = qi * tq
        hcols = slice(hh * LANES, (hh + 1) * LANES)
        qt = q_ref[lo:lo + tq, hcols].astype(F32).T
        row = lax.broadcasted_iota(jnp.int32, qt.shape, 0)
        qbd = jnp.concatenate([jnp.where(row < DIFF_DH, qt, 0.0), jnp.where(row >= DIFF_DH, qt, 0.0)],
                              axis=1).astype(BF16)
        sd = jnp.dot(k_ref[lo:lo + tq, hcols], qbd, preferred_element_type=F32) + bias_ref[...]
        m = jnp.max(sd, axis=0, keepdims=True)
        so = None
        if qi > 0:
            so = jnp.dot(k_ref[0:lo, hcols], qbd, preferred_element_type=F32)
            m = jnp.maximum(m, jnp.max(so, axis=0, keepdims=True))
        return sd, so, m

    def finish(hh, qi, sd, so, m):
        lo = qi * tq
        hcols = slice(hh * LANES, (hh + 1) * LANES)
        pd = jnp.exp2(sd - m)
        l = jnp.sum(pd, axis=0, keepdims=True)
        p = pd.astype(BF16)
        if qi > 0:
            po = jnp.exp2(so - m)
            l = l + jnp.sum(po, axis=0, keepdims=True)
            p = jnp.concatenate([po.astype(BF16), p], axis=0)
        on = jnp.dot(vt_ref[hh, :, 0:lo + tq], p, preferred_element_type=F32) / l
        o = (on[:, 0:tq] - lam * on[:, tq:2 * tq]).T
        ms_o = jnp.mean(o * o, axis=-1, keepdims=True)
        o = o * lax.rsqrt(ms_o + EPS) * sub_ref[...] * (1.0 - LAM_INIT)
        o_ref[lo:lo + tq, hcols] = (o * g_ref[lo:lo + tq, hcols].astype(F32)).astype(BF16)

    order = [(hh, qi) for hh in range(ATT_HEADS) for qi in reversed(range(s_len // tq))]
    return scores, finish, order


def _mixer_kernel(rq_ref, rk_ref, rv_ref, rg_ref, gn_ref, dmat_ref, xi_ref, zeta_ref, gdec_ref,
                  q_ref, k_ref, v_ref, g_ref, bias_ref, lq1_ref, lk1_ref, lq2_ref, lk2_ref, sub_ref,
                  wo_ref, wg_ref, wp_ref,
                  ro_ref, do_ref, wo_b_ref, wg_b_ref, wp_b_ref, vt_ref):
    wo_b_ref[...] = wo_ref[...].astype(BF16)
    wg_b_ref[...] = wg_ref[...].astype(BF16)
    wp_b_ref[...] = wp_ref[...].astype(BF16)

    r_front, r_back, r_advance, r_steps = _retention_stages(
        rq_ref, rk_ref, rv_ref, rg_ref, gn_ref, dmat_ref, xi_ref, zeta_ref, gdec_ref, ro_ref)
    a_scores, a_finish, order = _attention_stages(
        q_ref, k_ref, v_ref, g_ref, bias_ref, lq1_ref, lk1_ref, lq2_ref, lk2_ref, sub_ref, do_ref, vt_ref)
    every = len(order) // r_steps

    state = jnp.zeros((2 * RET_DK, 2 * RET_DV), F32)
    r_cur = r_front(0)
    a_cur = a_scores(*order[0])
    for n, unit in enumerate(order):
        a_nxt = a_scores(*order[n + 1]) if n + 1 < len(order) else None
        if n % every == 0:
            r_n = n // every
            r_nxt = r_front(r_n + 1) if r_n + 1 < r_steps else None
            kv, scs = r_cur
            r_back(r_n, scs, state)
            state = r_advance(state, kv)
            r_cur = r_nxt
        a_finish(*unit, *a_cur)
        a_cur = a_nxt


def _out_kernel(x_ref, ro_ref, do_ref, p_ref, wo_ref, pn_ref, wg_ref, wp_ref, o_ref):
    tm = x_ref.shape[0]

    def residual(rows):
        acc = jnp.dot(ro_ref[rows, :], wo_ref[0:RET_WIDTH, :], preferred_element_type=F32)
        acc = acc + jnp.dot(do_ref[rows, :], wo_ref[RET_WIDTH:, :], preferred_element_type=F32)
        h1 = x_ref[rows, :] + acc
        ms = jnp.mean(h1 * h1, axis=-1, keepdims=True)
        return h1, (h1 * lax.rsqrt(ms + EPS) * pn_ref[...]).astype(BF16)

    def gated(rows, h1, hn):
        z = jnp.dot(hn, wg_ref[...], preferred_element_type=F32)
        gate = 1.0 / (1.0 + jnp.exp(-z))
        ple = jnp.dot(p_ref[rows, :].astype(BF16), wp_ref[...], preferred_element_type=F32)
        o_ref[rows, :] = h1 + gate * ple

    chunks = [pl.ds(r * OUT_CHUNK, OUT_CHUNK) for r in range(tm // OUT_CHUNK)]
    cur = residual(chunks[0])
    for n, rows in enumerate(chunks):
        nxt = residual(chunks[n + 1]) if n + 1 < len(chunks) else None
        gated(rows, *cur)
        cur = nxt


def _decay_tables():
    log_g = np.log1p(-np.exp2(-5.0 - np.arange(RET_HEADS, dtype=np.float64)))
    idx = np.arange(RET_T, dtype=np.float64)
    dist = np.abs(idx[:, None] - idx[None, :])
    same_or_earlier = (np.arange(RET_T)[None, :] // CHUNK) <= (np.arange(RET_T)[:, None] // CHUNK)
    dmat = np.where(same_or_earlier[None], np.exp(dist[None] * log_g[:, None, None]), 0.0)
    xi = np.exp((idx + 1.0)[None, :] * log_g[:, None])
    zeta = np.exp((RET_T - 1.0 - idx)[None, :] * log_g[:, None])
    g_step = np.exp(RET_T * log_g)
    n_pair = RET_HEADS // 2
    widen = lambda t: np.broadcast_to(t.reshape(n_pair, 2, RET_T, 1), (n_pair, 2, RET_T, RET_DV)) \
        .transpose(0, 2, 1, 3).reshape(n_pair, RET_T, 2 * RET_DV)
    xi_w = widen(xi)
    zeta_w = widen(zeta)
    rowh = (np.arange(2 * RET_DK) // RET_DK)[:, None]
    colh = (np.arange(2 * RET_DV) // RET_DV)[None, :]
    diag = (rowh == colh).astype(np.float64)
    gp = g_step.reshape(n_pair, 2)
    gq = np.where(colh[None] == 0, gp[:, 0, None, None], gp[:, 1, None, None]) * diag[None]
    gdec = np.stack([gq, np.broadcast_to(diag[None], gq.shape)], axis=1)
    return tuple(jnp.asarray(t.astype(np.float32)) for t in (dmat, xi_w, zeta_w, gdec))


def _rope_tables(s_len):
    pos = np.arange(s_len, dtype=np.float64)
    inv_freq = ROPE_BASE ** (-np.arange(RET_DK // 2, dtype=np.float64) / (RET_DK // 2))
    ang = pos[:, None] * inv_freq[None, :]
    cos, sin = np.cos(ang), np.sin(ang)
    cos_t = np.tile(cos, (1, LANES // (RET_DK // 2)))
    sin_t = np.tile(np.concatenate([-sin, sin], axis=-1), (1, LANES // RET_DK))
    return jnp.asarray(cos_t.astype(np.float32)), jnp.asarray(sin_t.astype(np.float32))


def kernel(x, p, attn_norm, w_in, ret_gn, diff_qn, diff_kn, diff_lq1, diff_lk1, diff_lq2, diff_lk2, diff_subln, w_out, ple_norm, w_ple_gate, w_ple_proj):
    b, s, d = x.shape
    m = b * s
    assert d == D_MODEL and s % PROJ_TM == 0 and m % OUT_TM == 0 and s % RET_T == 0 and s % ATT_TQ == 0
    x2 = x.reshape(m, d)
    p2 = p[0].reshape(m, PLE_DIM)

    cos_t, sin_t = _rope_tables(s)
    dmat, xi_w, zeta_w, gdec = _decay_tables()
    qn_t = jnp.tile(diff_qn[0], LANES // DIFF_DH).reshape(1, LANES)
    kn_t = jnp.tile(diff_kn[0], LANES // DIFF_DH).reshape(1, LANES)

    params = functools.partial(pltpu.CompilerParams, vmem_limit_bytes=VMEM_LIMIT_BYTES)
    const = lambda *_: (0, 0)
    pos_blocks = s // PROJ_TM

    n_seg = D_IN // PROJ_TN
    n_row = m // PROJ_TM
    z, _ = pl.pallas_call(
        _proj_kernel,
        grid=(n_row, n_seg),
        in_specs=[
            pl.BlockSpec((PROJ_TM, d // 2), lambda i, j: (jnp.minimum(i + jnp.where(j >= 2, 1, 0), n_row - 1), 0)),
            pl.BlockSpec((PROJ_TM, d // 2), lambda i, j: (jnp.minimum(i + jnp.where(j >= 4, 1, 0), n_row - 1), 1)),
            pl.BlockSpec((1, d), const),
            pl.BlockSpec((d, PROJ_TN // 2), lambda i, j: (0, 2 * jnp.where(i == 0, j, n_seg - 1))),
            pl.BlockSpec((d, PROJ_TN // 2), lambda i, j: (0, 2 * jnp.where(i == 0, j, n_seg - 1) + 1)),
            pl.BlockSpec((PROJ_TM, LANES), lambda i, j: ((i + jnp.where(j >= 1, 1, 0)) % pos_blocks, 0)),
            pl.BlockSpec((PROJ_TM, LANES), lambda i, j: ((i + jnp.where(j >= 3, 1, 0)) % pos_blocks, 0)),
            pl.BlockSpec((1, LANES), const),
            pl.BlockSpec((1, LANES), const),
        ],
        out_specs=[pl.BlockSpec((None, PROJ_TM, PROJ_TN), lambda i, j: (j, i, 0)),
                   pl.BlockSpec(memory_space=pl.ANY)],
        out_shape=[jax.ShapeDtypeStruct((n_seg, m, PROJ_TN), BF16), jax.ShapeDtypeStruct((d, D_IN), BF16)],
        scratch_shapes=[pltpu.VMEM((PROJ_TM, d), BF16), pltpu.VMEM((2, d, PROJ_TN), BF16),
                        pltpu.SemaphoreType.DMA((2,)), pltpu.SemaphoreType.DMA((2,))],
        compiler_params=params(dimension_semantics=("arbitrary", "arbitrary")),
        name="proj",
    )(x2, x2, attn_norm[0].reshape(1, d), w_in[0], w_in[0], cos_t, sin_t, qn_t, kn_t)

    z4 = z.reshape(n_seg, b, s, PROJ_TN)

    def z_block(col0, width):
        seg, blk0 = col0 // PROJ_TN, (col0 % PROJ_TN) // width
        return pl.BlockSpec((None, None, s, width), lambda bi, h: (seg, bi, 0, blk0 + h))
    vec = lambda a: a[0].reshape(1, -1)
    tile_chunk = np.arange(ATT_TK) // CHUNK
    diag_bias = np.where(tile_chunk[:, None] <= tile_chunk[None, :], 0.0, NEG_BIG).astype(np.float32)
    diag_bias = jnp.asarray(np.tile(diag_bias, (1, 2)))
    hw = ATT_HEADS * LANES
    assert (DQ_BLK % ATT_HEADS, DK_BLK % ATT_HEADS, DV_BLK % ATT_HEADS, DG_BLK % ATT_HEADS) == (0, 0, 0, 0)
    n_att = DIFF_HEADS // ATT_HEADS
    assert n_att == RET_HEADS // 2
    w_slice = lambda rows: pl.BlockSpec((rows // (b * n_att), d), lambda bi, h: (bi * n_att + h, 0))
    ro, do, wo_b, wg_b, wp_b = pl.pallas_call(
        _mixer_kernel,
        grid=(b, n_att),
        in_specs=[
            z_block(RQ_BLK * LANES, LANES),
            z_block(RK_BLK * LANES, LANES),
            z_block(RV_COL, 2 * RET_DV),
            z_block(RG_COL, 2 * RET_DV),
            pl.BlockSpec((1, 2 * RET_DV), lambda bi, hp: (0, hp)),
            pl.BlockSpec((2, RET_T, RET_T), lambda bi, hp: (hp, 0, 0)),
            pl.BlockSpec((None, RET_T, 2 * RET_DV), lambda bi, hp: (hp, 0, 0)),
            pl.BlockSpec((None, RET_T, 2 * RET_DV), lambda bi, hp: (hp, 0, 0)),
            pl.BlockSpec((None, 2, 2 * RET_DK, 2 * RET_DV), lambda bi, hp: (hp, 0, 0, 0)),
            z_block(DQ_BLK * LANES, hw),
            z_block(DK_BLK * LANES, hw),
            z_block(DV_BLK * LANES, hw),
            z_block(DG_BLK * LANES, hw),
            pl.BlockSpec((ATT_TK, 2 * ATT_TQ), lambda *_: (0, 0)),
            pl.BlockSpec((1, DIFF_DH), lambda *_: (0, 0)),
            pl.BlockSpec((1, DIFF_DH), lambda *_: (0, 0)),
            pl.BlockSpec((1, DIFF_DH), lambda *_: (0, 0)),
            pl.BlockSpec((1, DIFF_DH), lambda *_: (0, 0)),
            pl.BlockSpec((1, DIFF_DV), lambda *_: (0, 0)),
            w_slice(d), w_slice(d), w_slice(PLE_DIM),
        ],
        out_specs=[pl.BlockSpec((None, s, 2 * RET_DV), lambda bi, hp: (bi, 0, hp)),
                   pl.BlockSpec((None, s, hw), lambda bi, h: (bi, 0, h)),
                   w_slice(d), w_slice(d), w_slice(PLE_DIM)],
        out_shape=[jax.ShapeDtypeStruct((b, s, RET_WIDTH), BF16),
                   jax.ShapeDtypeStruct((b, s, DIFF_WIDTH), BF16),
                   jax.ShapeDtypeStruct((d, d), BF16), jax.ShapeDtypeStruct((d, d), BF16),
                   jax.ShapeDtypeStruct((PLE_DIM, d), BF16)],
        scratch_shapes=[pltpu.VMEM((ATT_HEADS, DIFF_DV, s), BF16)],
        compiler_params=params(dimension_semantics=("arbitrary", "arbitrary")),
        name="mixer",
    )(z4, z4, z4, z4, ret_gn[0].reshape(1, RET_WIDTH), dmat, xi_w, zeta_w, gdec,
      z4, z4, z4, z4, diag_bias, vec(diff_lq1), vec(diff_lk1), vec(diff_lq2), vec(diff_lk2), vec(diff_subln),
      w_out[0], w_ple_gate[0], w_ple_proj[0])

    resident = functools.partial(pl.BlockSpec, pipeline_mode=pl.Buffered(1))
    out = pl.pallas_call(
        _out_kernel,
        grid=(m // OUT_TM,),
        in_specs=[
            pl.BlockSpec((OUT_TM, d), lambda i: (i, 0)),
            pl.BlockSpec((OUT_TM, RET_WIDTH), lambda i: (i, 0)),
            pl.BlockSpec((OUT_TM, DIFF_WIDTH), lambda i: (i, 0)),
            pl.BlockSpec((OUT_TM, PLE_DIM), lambda i: (i, 0)),
            resident((d, d), lambda i: (0, 0)),
            pl.BlockSpec((1, d), lambda i: (0, 0)),
            resident((d, d), lambda i: (0, 0)),
            resident((PLE_DIM, d), lambda i: (0, 0)),
        ],
        out_specs=pl.BlockSpec((OUT_TM, d), lambda i: (i, 0)),
        out_shape=jax.ShapeDtypeStruct((m, d), F32),
        compiler_params=params(dimension_semantics=("arbitrary",)),
        name="outproj",
    )(x2, ro.reshape(m, RET_WIDTH), do.reshape(m, DIFF_WIDTH), p2, wo_b, ple_norm[0].reshape(1, d), wg_b, wp_b)

    return out.reshape(b, s, d)
```

```python
import functools
import math

import jax
import jax.numpy as jnp
import numpy as np
from jax import lax
from jax.experimental import pallas as pl
from jax.experimental.pallas import tpu as pltpu

F32 = jnp.float32
BF16 = jnp.bfloat16

D_MODEL = 2048
CHUNK = 64
PLE_DIM = 256
RET_WIDTH = 1024
DIFF_WIDTH = 1024
RET_HEADS = 8
RET_DV = 128
RET_DK = 64
DIFF_HEADS = 8
DIFF_DV = 128
DIFF_DH = 64
ROPE_BASE = 10000.0
EPS = 1e-6
D_IN = 7168
LAM_INIT = 0.8 - 0.6 * math.exp(-0.3 * 0)

LANES = 128
VMEM_LIMIT_BYTES = 56 * 1024 * 1024

RQ_BLK = 0
RK_BLK = 512 // LANES
RV_COL = 1024
RG_COL = 2048
DQ_BLK = 3072 // LANES
DK_BLK = 4096 // LANES
DV_BLK = 5120 // LANES
DG_BLK = 6144 // LANES

PROJ_TM = 1024
PROJ_TN = 1024
ROW_CHUNK = 256
RET_T = 128
ATT_TQ = 256
ATT_TK = 256
ATT_HEADS = 2
OUT_TM = 512
OUT_CHUNK = 256
LOG2E = math.log2(math.e)
NEG_BIG = -1e30


def _silu(a):
    return a * (1.0 / (1.0 + jnp.exp(-a)))


def _swap_halves(a):
    lane = lax.broadcasted_iota(jnp.int32, a.shape, 1)
    first = (lane % RET_DK) < (RET_DK // 2)
    return jnp.where(first, pltpu.roll(a, LANES - RET_DK // 2, 1), pltpu.roll(a, RET_DK // 2, 1))


def _proj_kernel(xa_ref, xb_ref, g_ref, wl_ref, wr_ref, cos_ref, sin_ref, qn_ref, kn_ref, o_ref, wh_ref,
                 u_ref, wb_ref, rsem, wsem):
    i = pl.program_id(0)
    j = pl.program_id(1)
    n_seg = pl.num_programs(1)
    t = i * n_seg + j
    slot = lax.rem(t, 2)
    other = 1 - slot
    tm = xa_ref.shape[0]
    half = xa_ref.shape[1]
    wn = wl_ref.shape[1]
    tn = 2 * wn

    def wh_block(seg):
        return wh_ref.at[:, pl.ds(pl.multiple_of(seg * tn, tn), tn)]

    def write_copy(sl, seg):
        return pltpu.make_async_copy(wb_ref.at[sl], wh_block(seg), wsem.at[sl])

    def read_copy(sl, seg):
        return pltpu.make_async_copy(wh_block(seg), wb_ref.at[sl], rsem.at[sl])

    @pl.when(i == 0)
    def _():
        @pl.when(t >= 2)
        def _():
            write_copy(slot, j - 2).wait()

        for kt in range(wl_ref.shape[0] // ROW_CHUNK):
            krows = pl.ds(kt * ROW_CHUNK, ROW_CHUNK)
            wb_ref[slot, krows, 0:wn] = wl_ref[krows, :].astype(BF16)
            wb_ref[slot, krows, wn:tn] = wr_ref[krows, :].astype(BF16)
        write_copy(slot, j).start()

    @pl.when(i > 0)
    def _():
        read_copy(slot, j).wait()

    @pl.when((t + 1 >= n_seg) & (t + 1 < pl.num_programs(0) * n_seg))
    def _():
        @pl.when((t >= 1) & (t - 1 < n_seg))
        def _():
            write_copy(other, lax.rem(t - 1, n_seg)).wait()

        read_copy(other, lax.rem(j + 1, n_seg)).start()

    def rope_epilogue(a, rows):
        c = cos_ref[rows, :]
        s = sin_ref[rows, :]
        for sl in range(tn // LANES):
            scale = 1.0 if sl * LANES < RET_HEADS * RET_DK else RET_DK ** -0.5
            asl = a[:, sl * LANES:(sl + 1) * LANES]
            o_ref[rows, pl.ds(sl * LANES, LANES)] = ((asl * c + _swap_halves(asl) * s) * scale).astype(BF16)

    def norm_epilogue(gn_ref, scale):
        def apply(a, rows):
            low = lax.broadcasted_iota(jnp.int32, (1, LANES), 1) < DIFF_DH
            for sl in range(tn // LANES):
                asl = a[:, sl * LANES:(sl + 1) * LANES]
                sq = asl * asl
                s_lo = jnp.sum(jnp.where(low, sq, 0.0), axis=-1, keepdims=True)
                s_hi = jnp.sum(jnp.where(low, 0.0, sq), axis=-1, keepdims=True)
                ms = jnp.where(low, s_lo, s_hi) * (1.0 / DIFF_DH)
                o_ref[rows, pl.ds(sl * LANES, LANES)] = (asl * lax.rsqrt(ms + EPS) * gn_ref[...] * scale).astype(BF16)
        return apply

    def silu_epilogue(a, rows):
        o_ref[rows, :] = _silu(a).astype(BF16)

    def plain_epilogue(a, rows):
        o_ref[rows, :] = a.astype(BF16)

    def run(epilogue, normalize_input=False):
        sizes = [ROW_CHUNK] * (tm // ROW_CHUNK - 1) + [ROW_CHUNK // 2] * 2
        for r, size in enumerate(sizes):
            rows = pl.ds(sum(sizes[:r]), size)
            if normalize_input:
                xa = xa_ref[rows, :]
                xb = xb_ref[rows, :]
                ss = jnp.sum(xa * xa, axis=-1, keepdims=True) + jnp.sum(xb * xb, axis=-1, keepdims=True)
                inv = lax.rsqrt(ss * (1.0 / (2 * half)) + EPS)
                u_ref[rows, 0:half] = (xa * inv * g_ref[:, 0:half]).astype(BF16)
                u_ref[rows, half:2 * half] = (xb * inv * g_ref[:, half:2 * half]).astype(BF16)
            epilogue(jnp.dot(u_ref[rows, :], wb_ref[slot], preferred_element_type=F32), rows)

    @pl.when(j == 0)
    def _():
        run(rope_epilogue, normalize_input=True)

    @pl.when((j == 1) | (j == 5))
    def _():
        run(plain_epilogue)

    @pl.when((j == 2) | (j == 6))
    def _():
        run(silu_epilogue)

    @pl.when(j == 3)
    def _():
        run(norm_epilogue(qn_ref, DIFF_DH ** -0.5 * LOG2E))

    @pl.when(j == 4)
    def _():
        run(norm_epilogue(kn_ref, 1.0))


def _retention_stages(q_ref, k_ref, v_ref, g_ref, gn_ref, dmat_ref, xi_ref, zeta_ref, gdec_ref, o_ref):
    s_len = q_ref.shape[0]
    first_dk = lax.broadcasted_iota(jnp.int32, (1, 2 * RET_DK), 1) < RET_DK
    first_dv = lax.broadcasted_iota(jnp.int32, (1, 2 * RET_DV), 1) < RET_DV

    def pair_stack(a, first):
        return jnp.concatenate([jnp.where(first, a, jnp.zeros_like(a)), jnp.where(first, jnp.zeros_like(a), a)],
                               axis=0)

    def front(n):
        rows = pl.ds(n * RET_T, RET_T)
        q = q_ref[rows, :]
        k = k_ref[rows, :]
        vz = (v_ref[rows, :].astype(F32) * zeta_ref[...]).astype(BF16)
        kv = lax.dot_general(k, vz, (((0,), (0,)), ((), ())), preferred_element_type=F32)
        sc = lax.dot_general(q, pair_stack(k, first_dk), (((1,), (1,)), ((), ())), preferred_element_type=F32)
        return kv, (sc * dmat_ref[...]).astype(BF16)

    def back(n, sc, state):
        rows = pl.ds(n * RET_T, RET_T)
        cross = jnp.dot(q_ref[rows, :], state.astype(BF16), preferred_element_type=F32)
        intra = jnp.dot(sc, pair_stack(v_ref[rows, :], first_dv), preferred_element_type=F32)
        both = intra + cross * xi_ref[...]
        for hh in range(2):
            cols = pl.ds(hh * RET_DV, RET_DV)
            o = both[:, hh * RET_DV:(hh + 1) * RET_DV]
            ms = jnp.mean(o * o, axis=-1, keepdims=True)
            o = o * lax.rsqrt(ms + EPS) * gn_ref[:, cols]
            o_ref[rows, cols] = (o * g_ref[rows, cols].astype(F32)).astype(BF16)

    def advance(state, kv):
        return state * gdec_ref[0] + kv * gdec_ref[1]

    return front, back, advance, s_len // RET_T


def _attention_stages(q_ref, k_ref, v_ref, g_ref, bias_ref, lq1_ref, lk1_ref, lq2_ref, lk2_ref, sub_ref, o_ref,
                      vt_ref):
    s_len = k_ref.shape[0]
    tq = ATT_TQ
    for hh in range(ATT_HEADS):
        for t in range(s_len // ATT_TK):
            cols = pl.ds(t * ATT_TK, ATT_TK)
            vt_ref[hh, :, cols] = v_ref[cols, hh * DIFF_DV:(hh + 1) * DIFF_DV].astype(F32).T.astype(BF16)

    lam = (jnp.exp(jnp.sum(lq1_ref[...] * lk1_ref[...], axis=-1, keepdims=True))
           - jnp.exp(jnp.sum(lq2_ref[...] * lk2_ref[...], axis=-1, keepdims=True)) + LAM_INIT)

    def scores(hh, qi):
        lo = qi * tq
        hcols = slice(hh * LANES, (hh + 1) * LANES)
        qt = q_ref[lo:lo + tq, hcols].astype(F32).T
        row = lax.broadcasted_iota(jnp.int32, qt.shape, 0)
        qbd = jnp.concatenate([jnp.where(row < DIFF_DH, qt, 0.0), jnp.where(row >= DIFF_DH, qt, 0.0)],
                              axis=1).astype(BF16)
        sd = jnp.dot(k_ref[lo:lo + tq, hcols], qbd, preferred_element_type=F32) + bias_ref[...]
        m = jnp.max(sd, axis=0, keepdims=True)
        so = None
        if qi > 0:
            so = jnp.dot(k_ref[0:lo, hcols], qbd, preferred_element_type=F32)
            m = jnp.maximum(m, jnp.max(so, axis=0, keepdims=True))
        return sd, so, m

    def finish(hh, qi, sd, so, m):
        lo = qi * tq
        hcols = slice(hh * LANES, (hh + 1) * LANES)
        pd = jnp.exp2(sd - m)
        l = jnp.sum(pd, axis=0, keepdims=True)
        p = pd.astype(BF16)
        if qi > 0:
            po = jnp.exp2(so - m)
            l = l + jnp.sum(po, axis=0, keepdims=True)
            p = jnp.concatenate([po.astype(BF16), p], axis=0)
        on = jnp.dot(vt_ref[hh, :, 0:lo + tq], p, preferred_element_type=F32) / l
        o = (on[:, 0:tq] - lam * on[:, tq:2 * tq]).T
        ms_o = jnp.mean(o * o, axis=-1, keepdims=True)
        o = o * lax.rsqrt(ms_o + EPS) * sub_ref[...] * (1.0 - LAM_INIT)
        o_ref[lo:lo + tq, hcols] = (o * g_ref[lo:lo + tq, hcols].astype(F32)).astype(BF16)

    order = [(hh, qi) for hh in range(ATT_HEADS) for qi in reversed(range(s_len // tq))]
    return scores, finish, order


def _mixer_kernel(rq_ref, rk_ref, rv_ref, rg_ref, gn_ref, dmat_ref, xi_ref, zeta_ref, gdec_ref,
                  q_ref, k_ref, v_ref, g_ref, bias_ref, lq1_ref, lk1_ref, lq2_ref, lk2_ref, sub_ref,
                  wo_ref, wg_ref, wp_ref,
                  ro_ref, do_ref, wo_b_ref, wg_b_ref, wp_b_ref, vt_ref):
    wo_b_ref[...] = wo_ref[...].astype(BF16)
    wg_b_ref[...] = wg_ref[...].astype(BF16)
    wp_b_ref[...] = wp_ref[...].astype(BF16)

    r_front, r_back, r_advance, r_steps = _retention_stages(
        rq_ref, rk_ref, rv_ref, rg_ref, gn_ref, dmat_ref, xi_ref, zeta_ref, gdec_ref, ro_ref)
    a_scores, a_finish, order = _attention_stages(
        q_ref, k_ref, v_ref, g_ref, bias_ref, lq1_ref, lk1_ref, lq2_ref, lk2_ref, sub_ref, do_ref, vt_ref)
    every = len(order) // r_steps

    state = jnp.zeros((2 * RET_DK, 2 * RET_DV), F32)
    r_cur = r_front(0)
    a_cur = a_scores(*order[0])
    for n, unit in enumerate(order):
        a_nxt = a_scores(*order[n + 1]) if n + 1 < len(order) else None
        if n % every == 0:
            r_n = n // every
            r_nxt = r_front(r_n + 1) if r_n + 1 < r_steps else None
            kv, scs = r_cur
            r_back(r_n, scs, state)
            state = r_advance(state, kv)
            r_cur = r_nxt
        a_finish(*unit, *a_cur)
        a_cur = a_nxt


def _out_kernel(x_ref, ro_ref, do_ref, p_ref, wo_ref, pn_ref, wg_ref, wp_ref, o_ref):
    tm = x_ref.shape[0]

    def residual(rows):
        acc = jnp.dot(ro_ref[rows, :], wo_ref[0:RET_WIDTH, :], preferred_element_type=F32)
        acc = acc + jnp.dot(do_ref[rows, :], wo_ref[RET_WIDTH:, :], preferred_element_type=F32)
        h1 = x_ref[rows, :] + acc
        ms = jnp.mean(h1 * h1, axis=-1, keepdims=True)
        return h1, (h1 * lax.rsqrt(ms + EPS) * pn_ref[...]).astype(BF16)

    def gated(rows, h1, hn):
        z = jnp.dot(hn, wg_ref[...], preferred_element_type=F32)
        gate = 1.0 / (1.0 + jnp.exp(-z))
        ple = jnp.dot(p_ref[rows, :].astype(BF16), wp_ref[...], preferred_element_type=F32)
        o_ref[rows, :] = h1 + gate * ple

    chunks = [pl.ds(r * OUT_CHUNK, OUT_CHUNK) for r in range(tm // OUT_CHUNK)]
    cur = residual(chunks[0])
    for n, rows in enumerate(chunks):
        nxt = residual(chunks[n + 1]) if n + 1 < len(chunks) else None
        gated(rows, *cur)
        cur = nxt


def _decay_tables():
    log_g = np.log1p(-np.exp2(-5.0 - np.arange(RET_HEADS, dtype=np.float64)))
    idx = np.arange(RET_T, dtype=np.float64)
    dist = np.abs(idx[:, None] - idx[None, :])
    same_or_earlier = (np.arange(RET_T)[None, :] // CHUNK) <= (np.arange(RET_T)[:, None] // CHUNK)
    dmat = np.where(same_or_earlier[None], np.exp(dist[None] * log_g[:, None, None]), 0.0)
    xi = np.exp((idx + 1.0)[None, :] * log_g[:, None])
    zeta = np.exp((RET_T - 1.0 - idx)[None, :] * log_g[:, None])
    g_step = np.exp(RET_T * log_g)
    n_pair = RET_HEADS // 2
    widen = lambda t: np.broadcast_to(t.reshape(n_pair, 2, RET_T, 1), (n_pair, 2, RET_T, RET_DV)) \
        .transpose(0, 2, 1, 3).reshape(n_pair, RET_T, 2 * RET_DV)
    xi_w = widen(xi)
    zeta_w = widen(zeta)
    rowh = (np.arange(2 * RET_DK) // RET_DK)[:, None]
    colh = (np.arange(2 * RET_DV) // RET_DV)[None, :]
    diag = (rowh == colh).astype(np.float64)
    gp = g_step.reshape(n_pair, 2)
    gq = np.where(colh[None] == 0, gp[:, 0, None, None], gp[:, 1, None, None]) * diag[None]
    gdec = np.stack([gq, np.broadcast_to(diag[None], gq.shape)], axis=1)
    dmat_w = dmat.reshape(n_pair, 2, RET_T, RET_T).transpose(0, 2, 1, 3).reshape(n_pair, RET_T, 2 * RET_T)
    return tuple(jnp.asarray(t.astype(np.float32)) for t in (dmat_w, xi_w, zeta_w, gdec))


def _rope_tables(s_len):
    pos = np.arange(s_len, dtype=np.float64)
    inv_freq = ROPE_BASE ** (-np.arange(RET_DK // 2, dtype=np.float64) / (RET_DK // 2))
    ang = pos[:, None] * inv_freq[None, :]
    cos, sin = np.cos(ang), np.sin(ang)
    cos_t = np.tile(cos, (1, LANES // (RET_DK // 2)))
    sin_t = np.tile(np.concatenate([-sin, sin], axis=-1), (1, LANES // RET_DK))
    return jnp.asarray(cos_t.astype(np.float32)), jnp.asarray(sin_t.astype(np.float32))


def kernel(x, p, attn_norm, w_in, ret_gn, diff_qn, diff_kn, diff_lq1, diff_lk1, diff_lq2, diff_lk2, diff_subln, w_out, ple_norm, w_ple_gate, w_ple_proj):
    b, s, d = x.shape
    m = b * s
    assert d == D_MODEL and s % PROJ_TM == 0 and m % OUT_TM == 0 and s % RET_T == 0 and s % ATT_TQ == 0
    x2 = x.reshape(m, d)
    p2 = p[0].reshape(m, PLE_DIM)

    cos_t, sin_t = _rope_tables(s)
    dmat, xi_w, zeta_w, gdec = _decay_tables()
    qn_t = jnp.tile(diff_qn[0], LANES // DIFF_DH).reshape(1, LANES)
    kn_t = jnp.tile(diff_kn[0], LANES // DIFF_DH).reshape(1, LANES)

    params = functools.partial(pltpu.CompilerParams, vmem_limit_bytes=VMEM_LIMIT_BYTES)
    const = lambda *_: (0, 0)
    pos_blocks = s // PROJ_TM

    n_seg = D_IN // PROJ_TN
    n_row = m // PROJ_TM
    z, _ = pl.pallas_call(
        _proj_kernel,
        grid=(n_row, n_seg),
        in_specs=[
            pl.BlockSpec((PROJ_TM, d // 2), lambda i, j: (jnp.minimum(i + jnp.where(j >= 2, 1, 0), n_row - 1), 0)),
            pl.BlockSpec((PROJ_TM, d // 2), lambda i, j: (jnp.minimum(i + jnp.where(j >= 4, 1, 0), n_row - 1), 1)),
            pl.BlockSpec((1, d), const),
            pl.BlockSpec((d, PROJ_TN // 2), lambda i, j: (0, 2 * jnp.where(i == 0, j, n_seg - 1))),
            pl.BlockSpec((d, PROJ_TN // 2), lambda i, j: (0, 2 * jnp.where(i == 0, j, n_seg - 1) + 1)),
            pl.BlockSpec((PROJ_TM, LANES), lambda i, j: ((i + jnp.where(j >= 1, 1, 0)) % pos_blocks, 0)),
            pl.BlockSpec((PROJ_TM, LANES), lambda i, j: ((i + jnp.where(j >= 3, 1, 0)) % pos_blocks, 0)),
            pl.BlockSpec((1, LANES), const),
            pl.BlockSpec((1, LANES), const),
        ],
        out_specs=[pl.BlockSpec((None, PROJ_TM, PROJ_TN), lambda i, j: (j, i, 0)),
                   pl.BlockSpec(memory_space=pl.ANY)],
        out_shape=[jax.ShapeDtypeStruct((n_seg, m, PROJ_TN), BF16), jax.ShapeDtypeStruct((d, D_IN), BF16)],
        scratch_shapes=[pltpu.VMEM((PROJ_TM, d), BF16), pltpu.VMEM((2, d, PROJ_TN), BF16),
                        pltpu.SemaphoreType.DMA((2,)), pltpu.SemaphoreType.DMA((2,))],
        compiler_params=params(dimension_semantics=("arbitrary", "arbitrary")),
        name="proj",
    )(x2, x2, attn_norm[0].reshape(1, d), w_in[0], w_in[0], cos_t, sin_t, qn_t, kn_t)

    z4 = z.reshape(n_seg, b, s, PROJ_TN)

    def z_block(col0, width):
        seg, blk0 = col0 // PROJ_TN, (col0 % PROJ_TN) // width
        return pl.BlockSpec((None, None, s, width), lambda bi, h: (seg, bi, 0, blk0 + h))
    vec = lambda a: a[0].reshape(1, -1)
    tile_chunk = np.arange(ATT_TK) // CHUNK
    diag_bias = np.where(tile_chunk[:, None] <= tile_chunk[None, :], 0.0, NEG_BIG).astype(np.float32)
    diag_bias = jnp.asarray(np.tile(diag_bias, (1, 2)))
    hw = ATT_HEADS * LANES
    assert (DQ_BLK % ATT_HEADS, DK_BLK % ATT_HEADS, DV_BLK % ATT_HEADS, DG_BLK % ATT_HEADS) == (0, 0, 0, 0)
    n_att = DIFF_HEADS // ATT_HEADS
    assert n_att == RET_HEADS // 2
    w_slice = lambda rows: pl.BlockSpec((rows // (b * n_att), d), lambda bi, h: (bi * n_att + h, 0))
    ro, do, wo_b, wg_b, wp_b = pl.pallas_call(
        _mixer_kernel,
        grid=(b, n_att),
        in_specs=[
            z_block(RQ_BLK * LANES, LANES),
            z_block(RK_BLK * LANES, LANES),
            z_block(RV_COL, 2 * RET_DV),
            z_block(RG_COL, 2 * RET_DV),
            pl.BlockSpec((1, 2 * RET_DV), lambda bi, hp: (0, hp)),
            pl.BlockSpec((None, RET_T, 2 * RET_T), lambda bi, hp: (hp, 0, 0)),
            pl.BlockSpec((None, RET_T, 2 * RET_DV), lambda bi, hp: (hp, 0, 0)),
            pl.BlockSpec((None, RET_T, 2 * RET_DV), lambda bi, hp: (hp, 0, 0)),
            pl.BlockSpec((None, 2, 2 * RET_DK, 2 * RET_DV), lambda bi, hp: (hp, 0, 0, 0)),
            z_block(DQ_BLK * LANES, hw),
            z_block(DK_BLK * LANES, hw),
            z_block(DV_BLK * LANES, hw),
            z_block(DG_BLK * LANES, hw),
            pl.BlockSpec((ATT_TK, 2 * ATT_TQ), lambda *_: (0, 0)),
            pl.BlockSpec((1, DIFF_DH), lambda *_: (0, 0)),
            pl.BlockSpec((1, DIFF_DH), lambda *_: (0, 0)),
            pl.BlockSpec((1, DIFF_DH), lambda *_: (0, 0)),
            pl.BlockSpec((1, DIFF_DH), lambda *_: (0, 0)),
            pl.BlockSpec((1, DIFF_DV), lambda *_: (0, 0)),
            w_slice(d), w_slice(d), w_slice(PLE_DIM),
        ],
        out_specs=[pl.BlockSpec((None, s, 2 * RET_DV), lambda bi, hp: (bi, 0, hp)),
                   pl.BlockSpec((None, s, hw), lambda bi, h: (bi, 0, h)),
                   w_slice(d), w_slice(d), w_slice(PLE_DIM)],
        out_shape=[jax.ShapeDtypeStruct((b, s, RET_WIDTH), BF16),
                   jax.ShapeDtypeStruct((b, s, DIFF_WIDTH), BF16),
                   jax.ShapeDtypeStruct((d, d), BF16), jax.ShapeDtypeStruct((d, d), BF16),
                   jax.ShapeDtypeStruct((PLE_DIM, d), BF16)],
        scratch_shapes=[pltpu.VMEM((ATT_HEADS, DIFF_DV, s), BF16)],
        compiler_params=params(dimension_semantics=("arbitrary", "arbitrary")),
        name="mixer",
    )(z4, z4, z4, z4, ret_gn[0].reshape(1, RET_WIDTH), dmat, xi_w, zeta_w, gdec,
      z4, z4, z4, z4, diag_bias, vec(diff_lq1), vec(diff_lk1), vec(diff_lq2), vec(diff_lk2), vec(diff_subln),
      w_out[0], w_ple_gate[0], w_ple_proj[0])

    resident = functools.partial(pl.BlockSpec, pipeline_mode=pl.Buffered(1))
    out = pl.pallas_call(
        _out_kernel,
        grid=(m // OUT_TM,),
        in_specs=[
            pl.BlockSpec((OUT_TM, d), lambda i: (i, 0)),
            pl.BlockSpec((OUT_TM, RET_WIDTH), lambda i: (i, 0)),
            pl.BlockSpec((OUT_TM, DIFF_WIDTH), lambda i: (i, 0)),
            pl.BlockSpec((OUT_TM, PLE_DIM), lambda i: (i, 0)),
            resident((d, d), lambda i: (0, 0)),
            pl.BlockSpec((1, d), lambda i: (0, 0)),
            resident((d, d), lambda i: (0, 0)),
            resident((PLE_DIM, d), lambda i: (0, 0)),
        ],
        out_specs=pl.BlockSpec((OUT_TM, d), lambda i: (i, 0)),
        out_shape=jax.ShapeDtypeStruct((m, d), F32),
        compiler_params=params(dimension_semantics=("arbitrary",)),
        name="outproj",
    )(x2, ro.reshape(m, RET_WIDTH), do.reshape(m, DIFF_WIDTH), p2, wo_b, ple_norm[0].reshape(1, d), wg_b, wp_b)

    return out.reshape(b, s, d)
```

```python
import functools
import math

import jax
import jax.numpy as jnp
import numpy as np
from jax import lax
from jax.experimental import pallas as pl
from jax.experimental.pallas import tpu as pltpu

F32 = jnp.float32
BF16 = jnp.bfloat16

D_MODEL = 2048
CHUNK = 64
PLE_DIM = 256
RET_WIDTH = 1024
DIFF_WIDTH = 1024
RET_HEADS = 8
RET_DV = 128
RET_DK = 64
DIFF_HEADS = 8
DIFF_DV = 128
DIFF_DH = 64
ROPE_BASE = 10000.0
EPS = 1e-6
D_IN = 7168
LAM_INIT = 0.8 - 0.6 * math.exp(-0.3 * 0)

LANES = 128
VMEM_LIMIT_BYTES = 56 * 1024 * 1024

RQ_BLK = 0
RK_BLK = 512 // LANES
RV_COL = 1024
RG_COL = 2048
DQ_BLK = 3072 // LANES
DK_BLK = 4096 // LANES
DV_BLK = 5120 // LANES
DG_BLK = 6144 // LANES

PROJ_TM = 1024
PROJ_TN = 1024
ROW_CHUNK = 256
RET_T = 128
ATT_TQ = 256
ATT_TK = 256
ATT_HEADS = 2
OUT_TM = 512
OUT_CHUNK = 256
LOG2E = math.log2(math.e)
NEG_BIG = -1e30


def _silu(a):
    return a * (1.0 / (1.0 + jnp.exp(-a)))


def _swap_halves(a):
    lane = lax.broadcasted_iota(jnp.int32, a.shape, 1)
    first = (lane % RET_DK) < (RET_DK // 2)
    return jnp.where(first, pltpu.roll(a, LANES - RET_DK // 2, 1), pltpu.roll(a, RET_DK // 2, 1))


def _proj_kernel(xa_ref, xb_ref, g_ref, wl_ref, wr_ref, cos_ref, sin_ref, qn_ref, kn_ref, o_ref, wh_ref,
                 u_ref, wb_ref, rsem, wsem):
    i = pl.program_id(0)
    j = pl.program_id(1)
    n_seg = pl.num_programs(1)
    t = i * n_seg + j
    slot = lax.rem(t, 2)
    other = 1 - slot
    tm = xa_ref.shape[0]
    half = xa_ref.shape[1]
    wn = wl_ref.shape[1]
    tn = 2 * wn

    def wh_block(seg):
        return wh_ref.at[:, pl.ds(pl.multiple_of(seg * tn, tn), tn)]

    def write_copy(sl, seg):
        return pltpu.make_async_copy(wb_ref.at[sl], wh_block(seg), wsem.at[sl])

    def read_copy(sl, seg):
        return pltpu.make_async_copy(wh_block(seg), wb_ref.at[sl], rsem.at[sl])

    @pl.when(i == 0)
    def _():
        @pl.when(t >= 2)
        def _():
            write_copy(slot, j - 2).wait()

        for kt in range(wl_ref.shape[0] // ROW_CHUNK):
            krows = pl.ds(kt * ROW_CHUNK, ROW_CHUNK)
            wb_ref[slot, krows, 0:wn] = wl_ref[krows, :].astype(BF16)
            wb_ref[slot, krows, wn:tn] = wr_ref[krows, :].astype(BF16)
        write_copy(slot, j).start()

    @pl.when(i > 0)
    def _():
        read_copy(slot, j).wait()

    @pl.when((t + 1 >= n_seg) & (t + 1 < pl.num_programs(0) * n_seg))
    def _():
        @pl.when((t >= 1) & (t - 1 < n_seg))
        def _():
            write_copy(other, lax.rem(t - 1, n_seg)).wait()

        read_copy(other, lax.rem(j + 1, n_seg)).start()

    def rope_epilogue(a, rows):
        c = cos_ref[rows, :]
        s = sin_ref[rows, :]
        for sl in range(tn // LANES):
            scale = 1.0 if sl * LANES < RET_HEADS * RET_DK else RET_DK ** -0.5
            asl = a[:, sl * LANES:(sl + 1) * LANES]
            o_ref[rows, pl.ds(sl * LANES, LANES)] = ((asl * c + _swap_halves(asl) * s) * scale).astype(BF16)

    def norm_epilogue(gn_ref, scale):
        def apply(a, rows):
            low = lax.broadcasted_iota(jnp.int32, (1, LANES), 1) < DIFF_DH
            gain = jnp.concatenate([gn_ref[...]] * (LANES // DIFF_DH), axis=1)
            for sl in range(tn // LANES):
                asl = a[:, sl * LANES:(sl + 1) * LANES]
                sq = asl * asl
                s_lo = jnp.sum(jnp.where(low, sq, 0.0), axis=-1, keepdims=True)
                s_hi = jnp.sum(jnp.where(low, 0.0, sq), axis=-1, keepdims=True)
                ms = jnp.where(low, s_lo, s_hi) * (1.0 / DIFF_DH)
                o_ref[rows, pl.ds(sl * LANES, LANES)] = (asl * lax.rsqrt(ms + EPS) * gain * scale).astype(BF16)
        return apply

    def silu_epilogue(a, rows):
        o_ref[rows, :] = _silu(a).astype(BF16)

    def plain_epilogue(a, rows):
        o_ref[rows, :] = a.astype(BF16)

    def run(epilogue, normalize_input=False):
        sizes = [ROW_CHUNK] * (tm // ROW_CHUNK - 1) + [ROW_CHUNK // 2] * 2
        for r, size in enumerate(sizes):
            rows = pl.ds(sum(sizes[:r]), size)
            if normalize_input:
                xa = xa_ref[rows, :]
                xb = xb_ref[rows, :]
                ss = jnp.sum(xa * xa, axis=-1, keepdims=True) + jnp.sum(xb * xb, axis=-1, keepdims=True)
                inv = lax.rsqrt(ss * (1.0 / (2 * half)) + EPS)
                u_ref[rows, 0:half] = (xa * inv * g_ref[:, 0:half]).astype(BF16)
                u_ref[rows, half:2 * half] = (xb * inv * g_ref[:, half:2 * half]).astype(BF16)
            epilogue(jnp.dot(u_ref[rows, :], wb_ref[slot], preferred_element_type=F32), rows)

    @pl.when(j == 0)
    def _():
        run(rope_epilogue, normalize_input=True)

    @pl.when((j == 1) | (j == 5))
    def _():
        run(plain_epilogue)

    @pl.when((j == 2) | (j == 6))
    def _():
        run(silu_epilogue)

    @pl.when(j == 3)
    def _():
        run(norm_epilogue(qn_ref, DIFF_DH ** -0.5 * LOG2E))

    @pl.when(j == 4)
    def _():
        run(norm_epilogue(kn_ref, 1.0))


def _retention_stages(q_ref, k_ref, v_ref, g_ref, gn_ref, dmat_ref, xi_ref, zeta_ref, gdec_ref, o_ref):
    s_len = q_ref.shape[0]
    first_dk = lax.broadcasted_iota(jnp.int32, (1, 2 * RET_DK), 1) < RET_DK
    first_dv = lax.broadcasted_iota(jnp.int32, (1, 2 * RET_DV), 1) < RET_DV

    def pair_stack(a, first):
        return jnp.concatenate([jnp.where(first, a, jnp.zeros_like(a)), jnp.where(first, jnp.zeros_like(a), a)],
                               axis=0)

    def front(n):
        rows = pl.ds(n * RET_T, RET_T)
        q = q_ref[rows, :]
        k = k_ref[rows, :]
        vz = (v_ref[rows, :].astype(F32) * zeta_ref[...]).astype(BF16)
        kv = lax.dot_general(k, vz, (((0,), (0,)), ((), ())), preferred_element_type=F32)
        sc = lax.dot_general(q, pair_stack(k, first_dk), (((1,), (1,)), ((), ())), preferred_element_type=F32)
        return kv, (sc * dmat_ref[...]).astype(BF16)

    def back(n, sc, state):
        rows = pl.ds(n * RET_T, RET_T)
        cross = jnp.dot(q_ref[rows, :], state.astype(BF16), preferred_element_type=F32)
        intra = jnp.dot(sc, pair_stack(v_ref[rows, :], first_dv), preferred_element_type=F32)
        both = intra + cross * xi_ref[...]
        for hh in range(2):
            cols = pl.ds(hh * RET_DV, RET_DV)
            o = both[:, hh * RET_DV:(hh + 1) * RET_DV]
            ms = jnp.mean(o * o, axis=-1, keepdims=True)
            o = o * lax.rsqrt(ms + EPS) * gn_ref[:, cols]
            o_ref[rows, cols] = (o * g_ref[rows, cols].astype(F32)).astype(BF16)

    def advance(state, kv):
        return state * gdec_ref[0] + kv * gdec_ref[1]

    return front, back, advance, s_len // RET_T


def _attention_stages(q_ref, k_ref, v_ref, g_ref, bias_ref, lq1_ref, lk1_ref, lq2_ref, lk2_ref, sub_ref, o_ref,
                      vt_ref):
    s_len = k_ref.shape[0]
    tq = ATT_TQ
    for hh in range(ATT_HEADS):
        for t in range(s_len // ATT_TK):
            cols = pl.ds(t * ATT_TK, ATT_TK)
            vt_ref[hh, :, cols] = v_ref[cols, hh * DIFF_DV:(hh + 1) * DIFF_DV].astype(F32).T.astype(BF16)

    lam = (jnp.exp(jnp.sum(lq1_ref[...] * lk1_ref[...], axis=-1, keepdims=True))
           - jnp.exp(jnp.sum(lq2_ref[...] * lk2_ref[...], axis=-1, keepdims=True)) + LAM_INIT)

    def scores(hh, qi):
        lo = qi * tq
        hcols = slice(hh * LANES, (hh + 1) * LANES)
        qt = q_ref[lo:lo + tq, hcols].astype(F32).T
        row = lax.broadcasted_iota(jnp.int32, qt.shape, 0)
        qbd = jnp.concatenate([jnp.where(row < DIFF_DH, qt, 0.0), jnp.where(row >= DIFF_DH, qt, 0.0)],
                              axis=1).astype(BF16)
        sd = jnp.dot(k_ref[lo:lo + tq, hcols], qbd, preferred_element_type=F32) + bias_ref[...]
        m = jnp.max(sd, axis=0, keepdims=True)
        so = None
        if qi > 0:
            so = jnp.dot(k_ref[0:lo, hcols], qbd, preferred_element_type=F32)
            m = jnp.maximum(m, jnp.max(so, axis=0, keepdims=True))
        return sd, so, m

    def finish(hh, qi, sd, so, m):
        lo = qi * tq
        hcols = slice(hh * LANES, (hh + 1) * LANES)
        pd = jnp.exp2(sd - m)
        l = jnp.sum(pd, axis=0, keepdims=True)
        p = pd.astype(BF16)
        if qi > 0:
            po = jnp.exp2(so - m)
            l = l + jnp.sum(po, axis=0, keepdims=True)
            p = jnp.concatenate([po.astype(BF16), p], axis=0)
        on = jnp.dot(vt_ref[hh, :, 0:lo + tq], p, preferred_element_type=F32) / l
        o = (on[:, 0:tq] - lam * on[:, tq:2 * tq]).T
        ms_o = jnp.mean(o * o, axis=-1, keepdims=True)
        o = o * lax.rsqrt(ms_o + EPS) * sub_ref[...] * (1.0 - LAM_INIT)
        o_ref[lo:lo + tq, hcols] = (o * g_ref[lo:lo + tq, hcols].astype(F32)).astype(BF16)

    order = [(hh, qi) for hh in range(ATT_HEADS) for qi in reversed(range(s_len // tq))]
    return scores, finish, order


def _mixer_kernel(rq_ref, rk_ref, rv_ref, rg_ref, gn_ref, dmat_ref, xi_ref, zeta_ref, gdec_ref,
                  q_ref, k_ref, v_ref, g_ref, bias_ref, lq1_ref, lk1_ref, lq2_ref, lk2_ref, sub_ref,
                  wo_ref, wg_ref, wp_ref,
                  ro_ref, do_ref, wo_b_ref, wg_b_ref, wp_b_ref, vt_ref):
    wo_b_ref[...] = wo_ref[...].astype(BF16)
    wg_b_ref[...] = wg_ref[...].astype(BF16)
    wp_b_ref[...] = wp_ref[...].astype(BF16)

    r_front, r_back, r_advance, r_steps = _retention_stages(
        rq_ref, rk_ref, rv_ref, rg_ref, gn_ref, dmat_ref, xi_ref, zeta_ref, gdec_ref, ro_ref)
    a_scores, a_finish, order = _attention_stages(
        q_ref, k_ref, v_ref, g_ref, bias_ref, lq1_ref, lk1_ref, lq2_ref, lk2_ref, sub_ref, do_ref, vt_ref)
    every = len(order) // r_steps

    state = jnp.zeros((2 * RET_DK, 2 * RET_DV), F32)
    r_cur = r_front(0)
    a_cur = a_scores(*order[0])
    for n, unit in enumerate(order):
        a_nxt = a_scores(*order[n + 1]) if n + 1 < len(order) else None
        if n % every == 0:
            r_n = n // every
            r_nxt = r_front(r_n + 1) if r_n + 1 < r_steps else None
            kv, scs = r_cur
            r_back(r_n, scs, state)
            state = r_advance(state, kv)
            r_cur = r_nxt
        a_finish(*unit, *a_cur)
        a_cur = a_nxt


def _out_kernel(x_ref, ro_ref, do_ref, p_ref, wo_ref, pn_ref, wg_ref, wp_ref, o_ref):
    tm = x_ref.shape[0]

    def residual(rows):
        acc = jnp.dot(ro_ref[rows, :], wo_ref[0:RET_WIDTH, :], preferred_element_type=F32)
        acc = acc + jnp.dot(do_ref[rows, :], wo_ref[RET_WIDTH:, :], preferred_element_type=F32)
        h1 = x_ref[rows, :] + acc
        ms = jnp.mean(h1 * h1, axis=-1, keepdims=True)
        return h1, (h1 * lax.rsqrt(ms + EPS) * pn_ref[...]).astype(BF16)

    def gated(rows, h1, hn):
        z = jnp.dot(hn, wg_ref[...], preferred_element_type=F32)
        gate = 1.0 / (1.0 + jnp.exp(-z))
        ple = jnp.dot(p_ref[rows, :].astype(BF16), wp_ref[...], preferred_element_type=F32)
        o_ref[rows, :] = h1 + gate * ple

    chunks = [pl.ds(r * OUT_CHUNK, OUT_CHUNK) for r in range(tm // OUT_CHUNK)]
    cur = residual(chunks[0])
    for n, rows in enumerate(chunks):
        nxt = residual(chunks[n + 1]) if n + 1 < len(chunks) else None
        gated(rows, *cur)
        cur = nxt


def _decay_tables():
    log_g = np.log1p(-np.exp2(-5.0 - np.arange(RET_HEADS, dtype=np.float64)))
    idx = np.arange(RET_T, dtype=np.float64)
    dist = np.abs(idx[:, None] - idx[None, :])
    same_or_earlier = (np.arange(RET_T)[None, :] // CHUNK) <= (np.arange(RET_T)[:, None] // CHUNK)
    dmat = np.where(same_or_earlier[None], np.exp(dist[None] * log_g[:, None, None]), 0.0)
    xi = np.exp((idx + 1.0)[None, :] * log_g[:, None])
    zeta = np.exp((RET_T - 1.0 - idx)[None, :] * log_g[:, None])
    g_step = np.exp(RET_T * log_g)
    n_pair = RET_HEADS // 2
    widen = lambda t: np.broadcast_to(t.reshape(n_pair, 2, RET_T, 1), (n_pair, 2, RET_T, RET_DV)) \
        .transpose(0, 2, 1, 3).reshape(n_pair, RET_T, 2 * RET_DV)
    xi_w = widen(xi)
    zeta_w = widen(zeta)
    rowh = (np.arange(2 * RET_DK) // RET_DK)[:, None]
    colh = (np.arange(2 * RET_DV) // RET_DV)[None, :]
    diag = (rowh == colh).astype(np.float64)
    gp = g_step.reshape(n_pair, 2)
    gq = np.where(colh[None] == 0, gp[:, 0, None, None], gp[:, 1, None, None]) * diag[None]
    gdec = np.stack([gq, np.broadcast_to(diag[None], gq.shape)], axis=1)
    dmat_w = dmat.reshape(n_pair, 2, RET_T, RET_T).transpose(0, 2, 1, 3).reshape(n_pair, RET_T, 2 * RET_T)
    return tuple(jnp.asarray(t.astype(np.float32)) for t in (dmat_w, xi_w, zeta_w, gdec))


def _rope_tables(s_len):
    pos = np.arange(s_len, dtype=np.float64)
    inv_freq = ROPE_BASE ** (-np.arange(RET_DK // 2, dtype=np.float64) / (RET_DK // 2))
    ang = pos[:, None] * inv_freq[None, :]
    cos, sin = np.cos(ang), np.sin(ang)
    cos_t = np.tile(cos, (1, LANES // (RET_DK // 2)))
    sin_t = np.tile(np.concatenate([-sin, sin], axis=-1), (1, LANES // RET_DK))
    return jnp.asarray(cos_t.astype(np.float32)), jnp.asarray(sin_t.astype(np.float32))


def kernel(x, p, attn_norm, w_in, ret_gn, diff_qn, diff_kn, diff_lq1, diff_lk1, diff_lq2, diff_lk2, diff_subln, w_out, ple_norm, w_ple_gate, w_ple_proj):
    b, s, d = x.shape
    m = b * s
    assert d == D_MODEL and s % PROJ_TM == 0 and m % OUT_TM == 0 and s % RET_T == 0 and s % ATT_TQ == 0
    x2 = x.reshape(m, d)
    p2 = p[0].reshape(m, PLE_DIM)

    cos_t, sin_t = _rope_tables(s)
    dmat, xi_w, zeta_w, gdec = _decay_tables()

    params = functools.partial(pltpu.CompilerParams, vmem_limit_bytes=VMEM_LIMIT_BYTES)
    const = lambda *_: (0, 0)
    pos_blocks = s // PROJ_TM

    n_seg = D_IN // PROJ_TN
    n_row = m // PROJ_TM
    z, _ = pl.pallas_call(
        _proj_kernel,
        grid=(n_row, n_seg),
        in_specs=[
            pl.BlockSpec((PROJ_TM, d // 2), lambda i, j: (jnp.minimum(i + jnp.where(j >= 2, 1, 0), n_row - 1), 0)),
            pl.BlockSpec((PROJ_TM, d // 2), lambda i, j: (jnp.minimum(i + jnp.where(j >= 4, 1, 0), n_row - 1), 1)),
            pl.BlockSpec((1, d), const),
            pl.BlockSpec((d, PROJ_TN // 2), lambda i, j: (0, 2 * jnp.where(i == 0, j, n_seg - 1))),
            pl.BlockSpec((d, PROJ_TN // 2), lambda i, j: (0, 2 * jnp.where(i == 0, j, n_seg - 1) + 1)),
            pl.BlockSpec((PROJ_TM, LANES), lambda i, j: ((i + jnp.where(j >= 1, 1, 0)) % pos_blocks, 0)),
            pl.BlockSpec((PROJ_TM, LANES), lambda i, j: ((i + jnp.where(j >= 3, 1, 0)) % pos_blocks, 0)),
            pl.BlockSpec((1, DIFF_DH), const),
            pl.BlockSpec((1, DIFF_DH), const),
        ],
        out_specs=[pl.BlockSpec((None, PROJ_TM, PROJ_TN), lambda i, j: (j, i, 0)),
                   pl.BlockSpec(memory_space=pl.ANY)],
        out_shape=[jax.ShapeDtypeStruct((n_seg, m, PROJ_TN), BF16), jax.ShapeDtypeStruct((d, D_IN), BF16)],
        scratch_shapes=[pltpu.VMEM((PROJ_TM, d), BF16), pltpu.VMEM((2, d, PROJ_TN), BF16),
                        pltpu.SemaphoreType.DMA((2,)), pltpu.SemaphoreType.DMA((2,))],
        compiler_params=params(dimension_semantics=("arbitrary", "arbitrary")),
        name="proj",
    )(x2, x2, attn_norm[0].reshape(1, d), w_in[0], w_in[0], cos_t, sin_t, diff_qn, diff_kn)

    z4 = z.reshape(n_seg, b, s, PROJ_TN)

    def z_block(col0, width):
        seg, blk0 = col0 // PROJ_TN, (col0 % PROJ_TN) // width
        return pl.BlockSpec((None, None, s, width), lambda bi, h: (seg, bi, 0, blk0 + h))
    vec = lambda a: a[0].reshape(1, -1)
    tile_chunk = np.arange(ATT_TK) // CHUNK
    diag_bias = np.where(tile_chunk[:, None] <= tile_chunk[None, :], 0.0, NEG_BIG).astype(np.float32)
    diag_bias = jnp.asarray(np.tile(diag_bias, (1, 2)))
    hw = ATT_HEADS * LANES
    assert (DQ_BLK % ATT_HEADS, DK_BLK % ATT_HEADS, DV_BLK % ATT_HEADS, DG_BLK % ATT_HEADS) == (0, 0, 0, 0)
    n_att = DIFF_HEADS // ATT_HEADS
    assert n_att == RET_HEADS // 2
    w_slice = lambda rows: pl.BlockSpec((rows // (b * n_att), d), lambda bi, h: (bi * n_att + h, 0))
    ro, do, wo_b, wg_b, wp_b = pl.pallas_call(
        _mixer_kernel,
        grid=(b, n_att),
        in_specs=[
            z_block(RQ_BLK * LANES, LANES),
            z_block(RK_BLK * LANES, LANES),
            z_block(RV_COL, 2 * RET_DV),
            z_block(RG_COL, 2 * RET_DV),
            pl.BlockSpec((1, 2 * RET_DV), lambda bi, hp: (0, hp)),
            pl.BlockSpec((None, RET_T, 2 * RET_T), lambda bi, hp: (hp, 0, 0)),
            pl.BlockSpec((None, RET_T, 2 * RET_DV), lambda bi, hp: (hp, 0, 0)),
            pl.BlockSpec((None, RET_T, 2 * RET_DV), lambda bi, hp: (hp, 0, 0)),
            pl.BlockSpec((None, 2, 2 * RET_DK, 2 * RET_DV), lambda bi, hp: (hp, 0, 0, 0)),
            z_block(DQ_BLK * LANES, hw),
            z_block(DK_BLK * LANES, hw),
            z_block(DV_BLK * LANES, hw),
            z_block(DG_BLK * LANES, hw),
            pl.BlockSpec((ATT_TK, 2 * ATT_TQ), lambda *_: (0, 0)),
            pl.BlockSpec((1, DIFF_DH), lambda *_: (0, 0)),
            pl.BlockSpec((1, DIFF_DH), lambda *_: (0, 0)),
            pl.BlockSpec((1, DIFF_DH), lambda *_: (0, 0)),
            pl.BlockSpec((1, DIFF_DH), lambda *_: (0, 0)),
            pl.BlockSpec((1, DIFF_DV), lambda *_: (0, 0)),
            w_slice(d), w_slice(d), w_slice(PLE_DIM),
        ],
        out_specs=[pl.BlockSpec((None, s, 2 * RET_DV), lambda bi, hp: (bi, 0, hp)),
                   pl.BlockSpec((None, s, hw), lambda bi, h: (bi, 0, h)),
                   w_slice(d), w_slice(d), w_slice(PLE_DIM)],
        out_shape=[jax.ShapeDtypeStruct((b, s, RET_WIDTH), BF16),
                   jax.ShapeDtypeStruct((b, s, DIFF_WIDTH), BF16),
                   jax.ShapeDtypeStruct((d, d), BF16), jax.ShapeDtypeStruct((d, d), BF16),
                   jax.ShapeDtypeStruct((PLE_DIM, d), BF16)],
        scratch_shapes=[pltpu.VMEM((ATT_HEADS, DIFF_DV, s), BF16)],
        compiler_params=params(dimension_semantics=("arbitrary", "arbitrary")),
        name="mixer",
    )(z4, z4, z4, z4, ret_gn[0].reshape(1, RET_WIDTH), dmat, xi_w, zeta_w, gdec,
      z4, z4, z4, z4, diag_bias, vec(diff_lq1), vec(diff_lk1), vec(diff_lq2), vec(diff_lk2), vec(diff_subln),
      w_out[0], w_ple_gate[0], w_ple_proj[0])

    resident = functools.partial(pl.BlockSpec, pipeline_mode=pl.Buffered(1))
    out = pl.pallas_call(
        _out_kernel,
        grid=(m // OUT_TM,),
        in_specs=[
            pl.BlockSpec((OUT_TM, d), lambda i: (i, 0)),
            pl.BlockSpec((OUT_TM, RET_WIDTH), lambda i: (i, 0)),
            pl.BlockSpec((OUT_TM, DIFF_WIDTH), lambda i: (i, 0)),
            pl.BlockSpec((OUT_TM, PLE_DIM), lambda i: (i, 0)),
            resident((d, d), lambda i: (0, 0)),
            pl.BlockSpec((1, d), lambda i: (0, 0)),
            resident((d, d), lambda i: (0, 0)),
            resident((PLE_DIM, d), lambda i: (0, 0)),
        ],
        out_specs=pl.BlockSpec((OUT_TM, d), lambda i: (i, 0)),
        out_shape=jax.ShapeDtypeStruct((m, d), F32),
        compiler_params=params(dimension_semantics=("arbitrary",)),
        name="outproj",
    )(x2, ro.reshape(m, RET_WIDTH), do.reshape(m, DIFF_WIDTH), p2, wo_b, ple_norm[0].reshape(1, d), wg_b, wp_b)

    return out.reshape(b, s, d)
```
